```python
import jax
import jax.numpy as jnp
from jax import lax
import numpy as np

D_MODEL = 1024
BATCH = 32
SEQ = 2048
DEPTH = 4

CTX_LEN = 256
GRID_W = 64
ROPE_THETA = 10000.0
EPS = 1e-6
NEG_INF = -1e30
F32 = jnp.float32
ATTN_BLOCK = 128

MLA_HEADS = 4
MLA_Q_LORA = 256
MLA_KV_LORA = 256
MLA_NOPE = 128
MLA_ROPE = 64
MLA_V = 128
CMLP_GROUPS = 4
CMLP_GROUP_DIM = 128
CMLP_CHUNK = 128
CMLP_WIDTH = CMLP_GROUPS * CMLP_GROUP_DIM
RET_HEADS = 4
RET_QK = 64
RET_V = 128
RET_CHUNK = 128
RET_QK_W = RET_HEADS * RET_QK
RET_V_W = RET_HEADS * RET_V
SWA_Q_HEADS = 8
SWA_KV_HEADS = 2
SWA_HEAD_DIM = 64
SWA_WINDOW = 128
SWA_BLOCK = 128
SWA_Q_W = SWA_Q_HEADS * SWA_HEAD_DIM
SWA_KV_W = SWA_KV_HEADS * SWA_HEAD_DIM
FFN_HIDDEN = -(-8 * D_MODEL // (3 * 256)) * 256

AB_KV_WIDTH = MLA_KV_LORA + MLA_ROPE
AB_IN = AB_KV_WIDTH + MLA_Q_LORA + 2 * CMLP_WIDTH
AB_OUT = MLA_HEADS * MLA_V + CMLP_WIDTH
CD_KV_WIDTH = RET_QK_W + RET_V_W + 2 * SWA_KV_W
CD_IN = CD_KV_WIDTH + RET_QK_W + RET_V_W + SWA_Q_W
CD_OUT = RET_V_W + SWA_Q_W

N_EVEN = (DEPTH + 1) // 2
N_ODD = DEPTH // 2

kernel_name = 'hybrid_mla_gmlp_retention_swa_dit_trunk'


def _rms(x, g):
    xf = x.astype(F32)
    y = xf * lax.rsqrt(jnp.mean(xf * xf, axis=-1, keepdims=True) + EPS)
    return (y * g.astype(F32)).astype(x.dtype)


def _group_rms(y, g, groups):
    shp = y.shape
    yf = y.astype(F32).reshape(shp[:-1] + (groups, shp[-1] // groups))
    yf = yf * lax.rsqrt(jnp.mean(yf * yf, axis=-1, keepdims=True) + EPS)
    return (yf.reshape(shp) * g.astype(F32)).astype(y.dtype)


def _modulate(x, shift, scale):
    return x * (1.0 + scale) + shift


def _split_cols(z, widths):
    return jnp.split(z, np.cumsum(widths)[:-1].tolist(), axis=-1)


def _heads(a, h):
    return a.reshape(a.shape[:2] + (h, a.shape[-1] // h))


def _flip(a):
    return jnp.flip(a, axis=1)


def _axial_rope(n, rot_dim):
    t = jnp.arange(n)
    row = (t // GRID_W).astype(F32)
    col = (t % GRID_W).astype(F32)
    n_freq = rot_dim // 4
    freqs = ROPE_THETA ** (-jnp.arange(n_freq, dtype=F32) / n_freq)
    ang = jnp.concatenate([row[:, None] * freqs, col[:, None] * freqs], axis=-1)
    return jnp.cos(ang), jnp.sin(ang)


def _apply_rope(x, rope):
    cos, sin = rope
    cos = cos[None, :, None, :].astype(x.dtype)
    sin = sin[None, :, None, :].astype(x.dtype)
    x1, x2 = jnp.split(x, 2, axis=-1)
    return jnp.concatenate([x1 * cos - x2 * sin, x1 * sin + x2 * cos], axis=-1)


def _to_blocks(a, blk):
    B, n = a.shape[:2]
    return a.reshape((B, n // blk, blk) + a.shape[2:]).swapaxes(0, 1)


def _from_blocks(a):
    nb, B, blk = a.shape[:3]
    return a.swapaxes(0, 1).reshape((B, nb * blk) + a.shape[3:])


def _attend_block(q, k, v, scale, mask=None, sink=None):
    s = jnp.einsum('bqhgd,bkhd->bhgqk', q, k, preferred_element_type=F32) * scale
    if mask is not None:
        s = jnp.where(mask, s, NEG_INF)
    if sink is not None:
        sink_col = jnp.broadcast_to(sink.astype(F32)[None, :, :, None, None], s.shape[:-1] + (1,))
        p = jax.nn.softmax(jnp.concatenate([s, sink_col], axis=-1), axis=-1)[..., :-1]
    else:
        p = jax.nn.softmax(s, axis=-1)
    return jnp.einsum('bhgqk,bkhe->bqhge', p.astype(v.dtype), v)


def _dense_attention(q, k, v, scale, sink=None):
    out = lax.map(lambda qb: _attend_block(qb, k, v, scale, sink=sink), _to_blocks(q, ATTN_BLOCK))
    return _from_blocks(out)


def _window_attention(q, k, v, k_ctx, v_ctx, scale, sink):
    n = q.shape[1]
    blk = SWA_BLOCK
    nb = n // blk
    pad = ((0, 0), (blk, blk), (0, 0), (0, 0))
    k_pad = jnp.pad(k, pad)
    v_pad = jnp.pad(v, pad)
    rel_k = jnp.arange(3 * blk) - blk
    band = jnp.abs(rel_k[None, :] - jnp.arange(blk)[:, None]) <= SWA_WINDOW
    ctx_mask = jnp.ones((blk, k_ctx.shape[1]), dtype=bool)

    def one_block(args):
        i, qb = args
        kpos = i * blk + rel_k
        local = band & ((kpos >= 0) & (kpos < n))[None, :]
        kb = lax.dynamic_slice_in_dim(k_pad, i * blk, 3 * blk, axis=1)
        vb = lax.dynamic_slice_in_dim(v_pad, i * blk, 3 * blk, axis=1)
        return _attend_block(qb, jnp.concatenate([kb, k_ctx], axis=1), jnp.concatenate([vb, v_ctx], axis=1),
                             scale, jnp.concatenate([local, ctx_mask], axis=1), sink)

    out = lax.map(one_block, (jnp.arange(nb), _to_blocks(q, blk)))
    return _from_blocks(out)


def _retention_chunkwise(q, k, v, log_gamma, s0):
    C = RET_CHUNK
    idx = jnp.arange(C, dtype=F32)
    diff = idx[:, None] - idx[None, :]
    intra = jnp.where(diff[None] >= 0, jnp.exp(log_gamma[:, None, None] * jnp.maximum(diff, 0.0)[None]), 0.0)
    q_dec = jnp.exp((idx[:, None] + 1.0) * log_gamma[None, :])[None, :, :, None]
    k_dec = jnp.exp((C - 1.0 - idx[:, None]) * log_gamma[None, :])[None, :, :, None]
    c_dec = jnp.exp(C * log_gamma)[None, :, None, None]

    def step(s, inp):
        qc, kc, vc = inp
        vc = vc.astype(F32)
        att = jnp.einsum('bqhd,bkhd->bhqk', qc, kc, preferred_element_type=F32) * intra[None]
        y = jnp.einsum('bhqk,bkhe->bqhe', att, vc) + jnp.einsum('bqhd,bhde->bqhe', qc.astype(F32), s) * q_dec
        s = s * c_dec + jnp.einsum('bkhd,bkhe->bhde', kc.astype(F32) * k_dec, vc)
        return s, y

    s, ys = lax.scan(step, s0, (_to_blocks(q, C), _to_blocks(k, C), _to_blocks(v, C)))
    return _from_blocks(ys), s


def _retention_state(k, v, log_gamma):
    n = k.shape[1]
    w = jnp.exp((n - 1.0 - jnp.arange(n, dtype=F32))[:, None] * log_gamma[None, :])
    return jnp.einsum('bjhd,bjhe,jh->bhde', k.astype(F32), v.astype(F32), w)


def _mixer_ab(xn, hn, w_in, w_out, q_norm, kv_norm, wq_b, wkv_b, v_norm, w_s, b_s, with_ctx_out):
    n = xn.shape[1]
    rope = _axial_rope(n, MLA_ROPE)
    scale = (MLA_NOPE + MLA_ROPE) ** -0.5

    def kv_side(z, rope_):
        kv_lat, k_pe = _split_cols(z[..., :AB_KV_WIDTH], [MLA_KV_LORA, MLA_ROPE])
        kv = _heads(_rms(kv_lat, kv_norm) @ wkv_b, MLA_HEADS)
        k_nope, v = jnp.split(kv, [MLA_NOPE], axis=-1)
        k_pe = k_pe[:, :, None, :]
        if rope_ is not None:
            k_pe = _apply_rope(k_pe, rope_)
        k = jnp.concatenate([k_nope, jnp.broadcast_to(k_pe, k_nope.shape[:3] + (MLA_ROPE,))], axis=-1)
        return k, v

    def q_side(z, rope_):
        q_lat = z[..., AB_KV_WIDTH:AB_KV_WIDTH + MLA_Q_LORA]
        q = _heads(_rms(q_lat, q_norm) @ wq_b, MLA_HEADS)
        if rope_ is not None:
            q = jnp.concatenate([q[..., :MLA_NOPE], _apply_rope(q[..., MLA_NOPE:], rope_)], axis=-1)
        return q[:, :, :, None, :]

    def chunk_mlp(z):
        u, v = jnp.split(jax.nn.gelu(z[..., AB_KV_WIDTH + MLA_Q_LORA:]), 2, axis=-1)
        v = _group_rms(v, v_norm, CMLP_GROUPS)
        B, m = v.shape[:2]
        v = v.reshape(B, m // CMLP_CHUNK, CMLP_CHUNK, CMLP_GROUPS, CMLP_GROUP_DIM)
        v = jnp.einsum('gpq,bcqgd->bcpgd', w_s, v) + b_s.T[None, None, :, :, None]
        return u * v.reshape(B, m, CMLP_WIDTH)

    def merge(o, z):
        o = o.reshape(o.shape[:2] + (MLA_HEADS * MLA_V,))
        return jnp.concatenate([o, chunk_mlp(z)], axis=-1) @ w_out

    zx = xn @ w_in
    zh = hn @ (w_in if with_ctx_out else w_in[:, :AB_KV_WIDTH])
    kx, vx = kv_side(zx, rope)
    kh, vh = kv_side(zh, None)
    ox = _dense_attention(q_side(zx, rope), jnp.concatenate([kx, kh], axis=1),
                          jnp.concatenate([vx, vh], axis=1), scale)
    yx = merge(ox, zx)
    yh = None
    if with_ctx_out:
        yh = merge(_dense_attention(q_side(zh, None), kh, vh, scale), zh)
    return yx, yh


def _mixer_cd(xn, hn, w_in, w_out, dec_f, dec_b, ret_norm, sink, with_ctx_out):
    B, n = xn.shape[:2]
    rope_ret = _axial_rope(n, RET_QK)
    rope_swa = _axial_rope(n, SWA_HEAD_DIM)
    lg_f = jax.nn.log_sigmoid(dec_f.astype(F32))
    lg_b = jax.nn.log_sigmoid(dec_b.astype(F32))
    swa_scale = SWA_HEAD_DIM ** -0.5
    groups = SWA_Q_HEADS // SWA_KV_HEADS
    sink_g = sink.reshape(SWA_KV_HEADS, groups)

    def kv_side(z, use_rope):
        rk, rv, sk, sv = _split_cols(z[..., :CD_KV_WIDTH], [RET_QK_W, RET_V_W, SWA_KV_W, SWA_KV_W])
        rk = _heads(rk, RET_HEADS) * (RET_QK ** -0.5)
        sk = _heads(sk, SWA_KV_HEADS)
        if use_rope:
            rk = _apply_rope(rk, rope_ret)
            sk = _apply_rope(sk, rope_swa)
        return rk, _heads(rv, RET_HEADS), sk, _heads(sv, SWA_KV_HEADS)

    def q_side(z, use_rope):
        rq, rg, sq = _split_cols(z[..., CD_KV_WIDTH:], [RET_QK_W, RET_V_W, SWA_Q_W])
        rq = _heads(rq, RET_HEADS)
        sq = _heads(sq, SWA_Q_HEADS)
        if use_rope:
            rq = _apply_rope(rq, rope_ret)
            sq = _apply_rope(sq, rope_swa)
        return rq, rg, sq.reshape(sq.shape[:2] + (SWA_KV_HEADS, groups, SWA_HEAD_DIM))

    def merge(y_ret, gate, o_swa, dtype):
        y_ret = _group_rms(y_ret.reshape(y_ret.shape[:2] + (RET_V_W,)), ret_norm, RET_HEADS).astype(dtype)
        y_ret = y_ret * jax.nn.silu(gate)
        return jnp.concatenate([y_ret, o_swa.reshape(o_swa.shape[:2] + (SWA_Q_W,))], axis=-1) @ w_out

    zx = xn @ w_in
    zh = hn @ (w_in if with_ctx_out else w_in[:, :CD_KV_WIDTH])
    rkh, rvh, skh, svh = kv_side(zh, False)
    yh = None
    if with_ctx_out:
        rqh, rgh, sqh = q_side(zh, False)
        s0 = jnp.zeros((B, RET_HEADS, RET_QK, RET_V), F32)
        yh_f, s_f = _retention_chunkwise(rqh, rkh, rvh, lg_f, s0)
        yh_b, s_b = _retention_chunkwise(_flip(rqh), _flip(rkh), _flip(rvh), lg_b, s0)
        yh = merge(yh_f + _flip(yh_b), rgh, _dense_attention(sqh, skh, svh, swa_scale, sink_g), hn.dtype)
    else:
        s_f = _retention_state(rkh, rvh, lg_f)
        s_b = _retention_state(_flip(rkh), _flip(rvh), lg_b)
    rkx, rvx, skx, svx = kv_side(zx, True)
    rqx, rgx, sqx = q_side(zx, True)
    yx_f, _ = _retention_chunkwise(rqx, rkx, rvx, lg_f, s_f)
    yx_b, _ = _retention_chunkwise(_flip(rqx), _flip(rkx), _flip(rvx), lg_b, s_b)
    ox = _window_attention(sqx, skx, svx, skh, svh, swa_scale, sink_g)
    yx = merge(yx_f + _flip(yx_b), rgx, ox, xn.dtype)
    return yx, yh


def _swiglu(x, w_in, w_out):
    a, b = jnp.split(x @ w_in, 2, axis=-1)
    return (jax.nn.silu(a) * b) @ w_out


def setup_inputs(seed: int = 0) -> dict:
    key = jax.random.key(seed)
    ks = iter(jax.random.split(key, 32))
    D = D_MODEL

    def nrm(shape, s):
        return jax.random.normal(next(ks), shape, F32) * s

    decay_logit = jnp.asarray(np.log(2.0 ** (5 + np.arange(RET_HEADS)) - 1.0), dtype=F32)
    return dict(
        x=nrm((BATCH, SEQ, D), 1.0),
        c=nrm((BATCH, D), 1.0),
        ctx=nrm((BATCH, CTX_LEN, D), 1.0),
        c_ctx=nrm((D,), 1.0),
        ada_w=nrm((DEPTH, D, 6 * D), 0.5 * D ** -0.5),
        ada_b=nrm((DEPTH, 6 * D), 0.02),
        norm_mix=1.0 + nrm((DEPTH, D), 0.02),
        norm_ffn=1.0 + nrm((DEPTH, D), 0.02),
        norm_final=1.0 + nrm((D,), 0.02),
        ffn_in=nrm((DEPTH, D, 2 * FFN_HIDDEN), D ** -0.5),
        ffn_out=nrm((DEPTH, FFN_HIDDEN, D), FFN_HIDDEN ** -0.5),
        ab_in=nrm((N_EVEN, D, AB_IN), D ** -0.5),
        ab_out=nrm((N_EVEN, AB_OUT, D), AB_OUT ** -0.5),
        mla_q_norm=1.0 + nrm((N_EVEN, MLA_Q_LORA), 0.02),
        mla_kv_norm=1.0 + nrm((N_EVEN, MLA_KV_LORA), 0.02),
        mla_wq_b=nrm((N_EVEN, MLA_Q_LORA, MLA_HEADS * (MLA_NOPE + MLA_ROPE)), MLA_Q_LORA ** -0.5),
        mla_wkv_b=nrm((N_EVEN, MLA_KV_LORA, MLA_HEADS * (MLA_NOPE + MLA_V)), MLA_KV_LORA ** -0.5),
        cmlp_v_norm=1.0 + nrm((N_EVEN, CMLP_WIDTH), 0.02),
        cmlp_ws=nrm((N_EVEN, CMLP_GROUPS, CMLP_CHUNK, CMLP_CHUNK), CMLP_CHUNK ** -0.5),
        cmlp_bs=1.0 + nrm((N_EVEN, CMLP_GROUPS, CMLP_CHUNK), 0.02),
        cd_in=nrm((N_ODD, D, CD_IN), D ** -0.5),
        cd_out=nrm((N_ODD, CD_OUT, D), CD_OUT ** -0.5),
        ret_decay_fwd=decay_logit + nrm((N_ODD, RET_HEADS), 0.05),
        ret_decay_bwd=decay_logit + nrm((N_ODD, RET_HEADS), 0.05),
        ret_norm=1.0 + nrm((N_ODD, RET_V_W), 0.02),
        swa_sink=nrm((N_ODD, SWA_Q_HEADS), 0.5),
    )


def reference(x, c, ctx, c_ctx, ada_w, ada_b, norm_mix, norm_ffn, norm_final, ffn_in, ffn_out,
              ab_in, ab_out, mla_q_norm, mla_kv_norm, mla_wq_b, mla_wkv_b, cmlp_v_norm, cmlp_ws, cmlp_bs,
              cd_in, cd_out, ret_decay_fwd, ret_decay_bwd, ret_norm, swa_sink):
    h = ctx
    cond = jax.nn.silu(c)
    cond_ctx = jax.nn.silu(c_ctx)
    for layer in range(DEPTH):
        last = layer == DEPTH - 1
        mx = [m[:, None, :] for m in jnp.split(cond @ ada_w[layer] + ada_b[layer], 6, axis=-1)]
        mh = jnp.split(cond_ctx @ ada_w[layer] + ada_b[layer], 6, axis=-1)
        xn = _modulate(_rms(x, norm_mix[layer]), mx[0], mx[1])
        hn = _modulate(_rms(h, norm_mix[layer]), mh[0], mh[1])
        j = layer // 2
        if layer % 2 == 0:
            yx, yh = _mixer_ab(xn, hn, ab_in[j], ab_out[j], mla_q_norm[j], mla_kv_norm[j], mla_wq_b[j],
                               mla_wkv_b[j], cmlp_v_norm[j], cmlp_ws[j], cmlp_bs[j], not last)
        else:
            yx, yh = _mixer_cd(xn, hn, cd_in[j], cd_out[j], ret_decay_fwd[j], ret_decay_bwd[j],
                               ret_norm[j], swa_sink[j], not last)
        x = x + mx[2] * yx
        x = x + mx[5] * _swiglu(_modulate(_rms(x, norm_ffn[layer]), mx[3], mx[4]), ffn_in[layer], ffn_out[layer])
        if not last:
            h = h + mh[2] * yh
            h = h + mh[5] * _swiglu(_modulate(_rms(h, norm_ffn[layer]), mh[3], mh[4]), ffn_in[layer], ffn_out[layer])
    return _rms(x, norm_final)
```

```python
import functools

import numpy as np
import jax
import jax.numpy as jnp
from jax import lax
from jax.experimental import pallas as pl
from jax.experimental.pallas import tpu as pltpu

F32 = jnp.float32
BF16 = jnp.bfloat16

GRID_W = 64
ROPE_THETA = 10000.0
EPS = 1e-6
NEG_INF = -1e30

LANES = 128

MLA_HEADS = 4
MLA_LORA = 256
MLA_NOPE = 128
MLA_ROPE = 64
MLA_V = 128
MLA_QK_PAD = 256
CMLP_GROUPS = 4
CMLP_DIM = 128
CMLP_CHUNK = 128
CMLP_WIDTH = CMLP_GROUPS * CMLP_DIM
RET_HEADS = 4
RET_QK = 64
RET_V = 128
RET_CHUNK = 128
RET_QK_W = RET_HEADS * RET_QK
RET_V_W = RET_HEADS * RET_V
SWA_Q_HEADS = 8
SWA_KV_HEADS = 2
SWA_HEAD_DIM = 64
SWA_WINDOW = 128
SWA_Q_W = SWA_Q_HEADS * SWA_HEAD_DIM
SWA_KV_W = SWA_KV_HEADS * SWA_HEAD_DIM

VMEM_LIMIT_BYTES = 56 * 2**20

_NT = (((1,), (1,)), ((), ()))
_TN = (((0,), (0,)), ((), ()))


def _dot(a, b):
    return jnp.dot(a, b, preferred_element_type=F32)


def _dot_nt(a, b):
    return lax.dot_general(a, b, _NT, preferred_element_type=F32)


def _dot_tn(a, b):
    return lax.dot_general(a, b, _TN, preferred_element_type=F32)


def _rms(x, g):
    return x * lax.rsqrt(jnp.mean(x * x, axis=-1, keepdims=True) + EPS) * g


def _rms_mod(x, g, shift, scale):
    return _rms(x, g) * (1.0 + scale) + shift


def _rope128(x, cos, sin):
    lane = lax.broadcasted_iota(jnp.int32, x.shape, 1)
    first = (lane & 32) == 0
    swapped = jnp.where(first, pltpu.roll(x, LANES - 32, 1), pltpu.roll(x, 32, 1))
    return x * cos + swapped * sin


def _rope(x, cos, sin):
    cols = [_rope128(x[:, i:i + LANES], cos, sin) for i in range(0, x.shape[1], LANES)]
    return cols[0] if len(cols) == 1 else jnp.concatenate(cols, axis=-1)


def _params(n_grid):
    return pltpu.CompilerParams(dimension_semantics=("parallel",) * n_grid,
                                vmem_limit_bytes=VMEM_LIMIT_BYTES)


def _resident(shape):
    nd = len(shape)
    return pl.BlockSpec(shape, lambda *_: (0,) * nd, pipeline_mode=pl.Buffered(1))


def _smem():
    return pl.BlockSpec(memory_space=pltpu.SMEM)


def _tok_spec(tm, width):
    return pl.BlockSpec((1, tm, width), lambda b, i: (b, i, 0))


def _seq_spec(n, width):
    return pl.BlockSpec((1, n, width), lambda b, i: (b, 0, 0))


def _mod_spec(mod):
    if mod.shape[0] == 1:
        return pl.BlockSpec((1,) + mod.shape[1:], lambda b, i: (0, 0, 0))
    return pl.BlockSpec((1,) + mod.shape[1:], lambda b, i: (b, 0, 0))


def _pos_spec(tm):
    return pl.BlockSpec((tm, LANES), lambda b, i: (i, 0))


def _ada_kernel(c_ref, w_ref, b_ref, o_ref):
    cond = jax.nn.silu(c_ref[...]).astype(BF16)
    o_ref[0] = _dot(cond, w_ref[0].astype(BF16)) + b_ref[0]


def _ada(cond_rows, ada_w, ada_b):
    L, D, N = ada_w.shape
    R = cond_rows.shape[0]
    tn = 1536
    return pl.pallas_call(
        _ada_kernel,
        grid=(L, N // tn),
        in_specs=[pl.BlockSpec((R, D), lambda l, j: (0, 0)),
                  pl.BlockSpec((1, D, tn), lambda l, j: (l, 0, j)),
                  pl.BlockSpec((1, 1, tn), lambda l, j: (l, 0, j))],
        out_specs=pl.BlockSpec((1, R, tn), lambda l, j: (l, 0, j)),
        out_shape=jax.ShapeDtypeStruct((L, R, N), F32),
        compiler_params=_params(2),
        name="ada_mod",
    )(cond_rows, ada_w, ada_b.reshape(L, 1, N))


def _even_in_kernel(x_ref, mod_ref, g_ref, win_ref, wqn_ref, wqr_ref, wkn_ref, wv_ref,
                    qn_ref, kvn_ref, vn_ref, ws_ref, bs_ref, cos_ref, sin_ref,
                    q_out, k_out, v_out, cm_out, *, tm, scale):
    mod = mod_ref[0]
    xn = _rms_mod(x_ref[0], g_ref[...], mod[0:1], mod[1:2]).astype(BF16)
    cos = cos_ref[...]
    sin = sin_ref[...]
    L = MLA_LORA
    kv_lat = _dot(xn, win_ref[:, 0:L])
    q_lat = _dot(xn, win_ref[:, L:2 * L])
    uv_off = 2 * L
    pe_off = uv_off + 2 * CMLP_WIDTH
    kpe = _rope128(_dot(xn, win_ref[:, pe_off:pe_off + LANES]), cos, sin).astype(BF16)
    kvn = _rms(kv_lat, kvn_ref[...]).astype(BF16)
    qn = _rms(q_lat, qn_ref[...]).astype(BF16)
    k_nope = _dot(kvn, wkn_ref[...])
    v_out[0] = _dot(kvn, wv_ref[...]).astype(BF16)
    q_nope = _dot(qn, wqn_ref[...]) * scale
    q_rope = _dot(qn, wqr_ref[...]) * scale
    for h in range(MLA_HEADS):
        src = slice(LANES * h, LANES * (h + 1))
        lo = MLA_QK_PAD * h
        q_out[0, :, lo:lo + LANES] = q_nope[:, src].astype(BF16)
        q_out[0, :, lo + LANES:lo + 2 * LANES] = _rope128(q_rope[:, src], cos, sin).astype(BF16)
        k_out[0, :, lo:lo + LANES] = k_nope[:, src].astype(BF16)
        k_out[0, :, lo + LANES:lo + 2 * LANES] = kpe
    uv = jax.nn.gelu(_dot(xn, win_ref[:, uv_off:uv_off + 2 * CMLP_WIDTH]))
    for g in range(CMLP_GROUPS):
        col = slice(CMLP_DIM * g, CMLP_DIM * (g + 1))
        vg = uv[:, CMLP_WIDTH + CMLP_DIM * g:CMLP_WIDTH + CMLP_DIM * (g + 1)]
        vg = _rms(vg, vn_ref[:, col]).astype(BF16)
        ug = uv[:, col]
        wsg = ws_ref[g]
        bsg = bs_ref[g]
        for c in range(tm // CMLP_CHUNK):
            rows = slice(CMLP_CHUNK * c, CMLP_CHUNK * (c + 1))
            mixed = _dot(wsg, vg[rows]) + bsg
            cm_out[0, rows, col] = (ug[rows] * mixed).astype(BF16)


def _even_in(xs, mod, g, w, cos, sin, tm):
    B, n, D = xs.shape
    tm = min(tm, n)
    qk_w = MLA_HEADS * MLA_QK_PAD
    v_w = MLA_HEADS * MLA_V
    scale = float((MLA_NOPE + MLA_ROPE) ** -0.5)
    weights = (w["w_in"], w["wqn"], w["wqr"], w["wkn"], w["wv"], w["q_norm"], w["kv_norm"],
               w["v_norm"], w["ws"], w["bs"])
    return pl.pallas_call(
        functools.partial(_even_in_kernel, tm=tm, scale=scale),
        grid=(B, n // tm),
        in_specs=[_tok_spec(tm, D), _mod_spec(mod), _resident(g.shape)]
                 + [_resident(a.shape) for a in weights] + [_pos_spec(tm), _pos_spec(tm)],
        out_specs=[_tok_spec(tm, qk_w), _tok_spec(tm, qk_w), _tok_spec(tm, v_w), _tok_spec(tm, CMLP_WIDTH)],
        out_shape=[jax.ShapeDtypeStruct((B, n, qk_w), BF16), jax.ShapeDtypeStruct((B, n, qk_w), BF16),
                   jax.ShapeDtypeStruct((B, n, v_w), BF16), jax.ShapeDtypeStruct((B, n, CMLP_WIDTH), BF16)],
        compiler_params=_params(2),
        name="even_in",
    )(xs, mod, g, *weights, cos, sin)


def _mla_attn_kernel(*refs, n_pieces):
    q_ref = refs[0]
    kv_refs = refs[1:1 + 2 * n_pieces]
    cm_ref, x_ref, mod_ref, wout_ref, o_ref = refs[1 + 2 * n_pieces:]
    q = q_ref[0]
    heads = []
    for h in range(MLA_HEADS):
        qk = slice(MLA_QK_PAD * h, MLA_QK_PAD * (h + 1))
        vc = slice(MLA_V * h, MLA_V * (h + 1))
        qh = q[:, qk]
        scores = [_dot_nt(qh, kv_refs[2 * p][0, :, qk]) for p in range(n_pieces)]
        m = functools.reduce(jnp.maximum, [jnp.max(s, axis=-1, keepdims=True) for s in scores])
        probs = [jnp.exp(s - m) for s in scores]
        denom = sum(jnp.sum(p, axis=-1, keepdims=True) for p in probs)
        o = sum(_dot(probs[p].astype(BF16), kv_refs[2 * p + 1][0, :, vc]) for p in range(n_pieces))
        heads.append(o / denom)
    o = jnp.concatenate(heads, axis=-1).astype(BF16)
    ow = MLA_HEADS * MLA_V
    y = _dot(o, wout_ref[0:ow, :]) + _dot(cm_ref[0], wout_ref[ow:ow + CMLP_WIDTH, :])
    o_ref[0] = x_ref[0] + mod_ref[0][2:3] * y


def _mla_attn(q, pieces, cm, xs, mod, w_out, tq):
    B, n, D = xs.shape
    tq = min(tq, n)
    kv_specs, kv_args = [], []
    for k, v in pieces:
        kv_specs += [_seq_spec(k.shape[1], k.shape[2]), _seq_spec(v.shape[1], v.shape[2])]
        kv_args += [k, v]
    return pl.pallas_call(
        functools.partial(_mla_attn_kernel, n_pieces=len(pieces)),
        grid=(B, n // tq),
        in_specs=[_tok_spec(tq, q.shape[2])] + kv_specs
                 + [_tok_spec(tq, cm.shape[2]), _tok_spec(tq, D), _mod_spec(mod), _resident(w_out.shape)],
        out_specs=_tok_spec(tq, D),
        out_shape=jax.ShapeDtypeStruct((B, n, D), F32),
        compiler_params=_params(2),
        name="mla_attn",
    )(q, *kv_args, cm, xs, mod, w_out)


def _ffn_kernel(*refs, hidden, th, final):
    if final:
        x_ref, mod_ref, g_ref, win_ref, wout_ref, gf_ref, o_ref = refs
    else:
        x_ref, mod_ref, g_ref, win_ref, wout_ref, o_ref = refs
    x = x_ref[0]
    mod = mod_ref[0]
    xn = _rms_mod(x, g_ref[...], mod[3:4], mod[4:5]).astype(BF16)
    acc = None
    for c in range(hidden // th):
        a = _dot(xn, win_ref[:, c * th:(c + 1) * th])
        b = _dot(xn, win_ref[:, hidden + c * th:hidden + (c + 1) * th])
        d = _dot((jax.nn.silu(a) * b).astype(BF16), wout_ref[c * th:(c + 1) * th, :])
        acc = d if acc is None else acc + d
    out = x + mod[5:6] * acc
    if final:
        out = _rms(out, gf_ref[...])
    o_ref[0] = out


def _ffn(xs, mod, g, w_in, w_out, tm, g_final=None):
    B, n, D = xs.shape
    tm = min(tm, n)
    hidden = w_out.shape[0]
    th = hidden // 2
    final = g_final is not None
    extra = [g_final] if final else []
    return pl.pallas_call(
        functools.partial(_ffn_kernel, hidden=hidden, th=th, final=final),
        grid=(B, n // tm),
        in_specs=[_tok_spec(tm, D), _mod_spec(mod), _resident(g.shape), _resident(w_in.shape),
                  _resident(w_out.shape)] + [_resident(a.shape) for a in extra],
        out_specs=_tok_spec(tm, D),
        out_shape=jax.ShapeDtypeStruct((B, n, D), F32),
        compiler_params=_params(2),
        name="ffn",
    )(xs, mod, g, w_in, w_out, *extra)


def _odd_in_kernel(x_ref, mod_ref, g_ref, w_ref, cos_ref, sin_ref,
                   rq_out, rk_out, rv_out, rg_out, sq_out, sk_out, sv_out):
    mod = mod_ref[0]
    xn = _rms_mod(x_ref[0], g_ref[...], mod[0:1], mod[1:2]).astype(BF16)
    cos = cos_ref[...]
    sin = sin_ref[...]

    def proj(lo, width):
        return _dot(xn, w_ref[:, lo:lo + width])

    o = 0
    rk_out[0] = (_rope(proj(o, RET_QK_W), cos, sin) * (RET_QK ** -0.5)).astype(BF16)
    o += RET_QK_W
    rv_out[0] = proj(o, RET_V_W).astype(BF16)
    o += RET_V_W
    sk_out[0] = _rope(proj(o, SWA_KV_W), cos, sin).astype(BF16)
    o += SWA_KV_W
    sv_out[0] = proj(o, SWA_KV_W).astype(BF16)
    o += SWA_KV_W
    rq_out[0] = _rope(proj(o, RET_QK_W), cos, sin).astype(BF16)
    o += RET_QK_W
    rg_out[0] = proj(o, RET_V_W)
    o += RET_V_W
    sq_out[0] = (_rope(proj(o, SWA_Q_W), cos, sin) * (SWA_HEAD_DIM ** -0.5)).astype(BF16)


def _odd_in(xs, mod, g, w_in, cos, sin, tm):
    B, n, D = xs.shape
    tm = min(tm, n)
    widths = [(RET_QK_W, BF16), (RET_QK_W, BF16), (RET_V_W, BF16), (RET_V_W, F32),
              (SWA_Q_W, BF16), (SWA_KV_W, BF16), (SWA_KV_W, BF16)]
    return pl.pallas_call(
        _odd_in_kernel,
        grid=(B, n // tm),
        in_specs=[_tok_spec(tm, D), _mod_spec(mod), _resident(g.shape), _resident(w_in.shape),
                  _pos_spec(tm), _pos_spec(tm)],
        out_specs=[_tok_spec(tm, w) for w, _ in widths],
        out_shape=[jax.ShapeDtypeStruct((B, n, w), dt) for w, dt in widths],
        compiler_params=_params(2),
        name="odd_in",
    )(xs, mod, g, w_in, cos, sin)


def _ret_kernel(lg_ref, rq_ref, rk_ref, rv_ref, rg_ref, sf0_ref, sb0_ref, rn_ref,
                yr_ref, sfo_ref, sbo_ref, sf_scr, *, nc):
    C = RET_CHUNK
    row = lax.broadcasted_iota(jnp.int32, (C, C), 0)
    col = lax.broadcasted_iota(jnp.int32, (C, C), 1)
    diff = (row - col).astype(F32)
    pos = lax.broadcasted_iota(jnp.int32, (C, 1), 0).astype(F32)
    for h in range(RET_HEADS):
        lf = lg_ref[0, h]
        lb = lg_ref[1, h]
        qc = slice(RET_QK * h, RET_QK * (h + 1))
        vc = slice(RET_V * h, RET_V * (h + 1))
        intra = (jnp.where(diff >= 0, jnp.exp(lf * jnp.maximum(diff, 0.0)), 0.0)
                 + jnp.where(diff <= 0, jnp.exp(lb * jnp.maximum(-diff, 0.0)), 0.0))
        q_dec_f = jnp.exp((pos + 1.0) * lf)
        k_dec_f = jnp.exp((C - 1.0 - pos) * lf)
        q_dec_b = jnp.exp((C - pos) * lb)
        k_dec_b = jnp.exp(pos * lb)
        c_dec_f = jnp.exp(jnp.full((1, 1), C, F32) * lf)
        c_dec_b = jnp.exp(jnp.full((1, 1), C, F32) * lb)

        def fwd(c, s, qc=qc, vc=vc, k_dec_f=k_dec_f, c_dec_f=c_dec_f, h=h):
            sf_scr[c, h] = s
            rows = pl.ds(pl.multiple_of(c * C, C), C)
            k = (rk_ref[0, rows, qc].astype(F32) * k_dec_f).astype(BF16)
            return s * c_dec_f + _dot_tn(k, rv_ref[0, rows, vc])

        sfo_ref[0, h] = lax.fori_loop(0, nc, fwd, sf0_ref[0, h])

        def bwd(i, s, qc=qc, vc=vc, intra=intra, q_dec_f=q_dec_f, q_dec_b=q_dec_b,
                k_dec_b=k_dec_b, c_dec_b=c_dec_b, h=h):
            c = nc - 1 - i
            rows = pl.ds(pl.multiple_of(c * C, C), C)
            q = rq_ref[0, rows, qc]
            k = rk_ref[0, rows, qc]
            v = rv_ref[0, rows, vc]
            att = (_dot_nt(q, k) * intra).astype(BF16)
            y = (_dot(att, v) + _dot(q, sf_scr[c, h].astype(BF16)) * q_dec_f
                 + _dot(q, s.astype(BF16)) * q_dec_b)
            y = _rms(y, rn_ref[:, vc]) * jax.nn.silu(rg_ref[0, rows, vc])
            yr_ref[0, rows, vc] = y.astype(BF16)
            kd = (k.astype(F32) * k_dec_b).astype(BF16)
            return s * c_dec_b + _dot_tn(kd, v)

        sbo_ref[0, h] = lax.fori_loop(0, nc, bwd, sb0_ref[0, h])


def _retention(lg, rq, rk, rv, rg, sf0, sb0, ret_norm):
    B, n, _ = rq.shape
    nc = n // RET_CHUNK
    st_shape = (B, RET_HEADS, RET_QK, RET_V)
    st_spec = pl.BlockSpec((1, RET_HEADS, RET_QK, RET_V), lambda b: (b, 0, 0, 0))

    def seq(width):
        return pl.BlockSpec((1, n, width), lambda b: (b, 0, 0))

    return pl.pallas_call(
        functools.partial(_ret_kernel, nc=nc),
        grid=(B,),
        in_specs=[_smem(), seq(RET_QK_W), seq(RET_QK_W), seq(RET_V_W), seq(RET_V_W), st_spec, st_spec,
                  _resident(ret_norm.shape)],
        out_specs=[seq(RET_V_W), st_spec, st_spec],
        out_shape=[jax.ShapeDtypeStruct((B, n, RET_V_W), BF16), jax.ShapeDtypeStruct(st_shape, F32),
                   jax.ShapeDtypeStruct(st_shape, F32)],
        scratch_shapes=[pltpu.VMEM((nc, RET_HEADS, RET_QK, RET_V), F32)],
        compiler_params=_params(1),
        name="retention",
    )(lg, rq, rk, rv, rg, sf0, sb0, ret_norm)


def _swa_kernel(*refs, tq, n, local):
    if local:
        sink_ref, sq_ref, skx_ref, svx_ref, skh_ref, svh_ref, yr_ref, x_ref, mod_ref, wout_ref, o_ref = refs
    else:
        sink_ref, sq_ref, skh_ref, svh_ref, yr_ref, x_ref, mod_ref, wout_ref, o_ref = refs
    q = sq_ref[0]
    kh = skh_ref[0]
    vh = svh_ref[0]
    if local:
        i = pl.program_id(1)
        win = tq + 2 * SWA_WINDOW
        start = pl.multiple_of(jnp.clip(i * tq - SWA_WINDOW, 0, n - win), SWA_WINDOW)
        kx = skx_ref[0, pl.ds(start, win), :]
        vx = svx_ref[0, pl.ds(start, win), :]
        qpos = i * tq + lax.broadcasted_iota(jnp.int32, (tq, win), 0)
        kpos = start + lax.broadcasted_iota(jnp.int32, (tq, win), 1)
        band = jnp.abs(qpos - kpos) <= SWA_WINDOW
    groups = SWA_Q_HEADS // SWA_KV_HEADS
    heads = []
    for h in range(SWA_Q_HEADS):
        kvc = slice(SWA_HEAD_DIM * (h // groups), SWA_HEAD_DIM * (h // groups + 1))
        qh = q[:, SWA_HEAD_DIM * h:SWA_HEAD_DIM * (h + 1)]
        sink = sink_ref[h]
        s_h = _dot_nt(qh, kh[:, kvc])
        m = jnp.maximum(jnp.max(s_h, axis=-1, keepdims=True), sink)
        if local:
            s_x = jnp.where(band, _dot_nt(qh, kx[:, kvc]), NEG_INF)
            m = jnp.maximum(m, jnp.max(s_x, axis=-1, keepdims=True))
        p_h = jnp.exp(s_h - m)
        denom = jnp.sum(p_h, axis=-1, keepdims=True) + jnp.exp(sink - m)
        o = _dot(p_h.astype(BF16), vh[:, kvc])
        if local:
            p_x = jnp.exp(s_x - m)
            denom = denom + jnp.sum(p_x, axis=-1, keepdims=True)
            o = o + _dot(p_x.astype(BF16), vx[:, kvc])
        heads.append(o / denom)
    o = jnp.concatenate(heads, axis=-1).astype(BF16)
    y = _dot(yr_ref[0], wout_ref[0:RET_V_W, :]) + _dot(o, wout_ref[RET_V_W:RET_V_W + SWA_Q_W, :])
    o_ref[0] = x_ref[0] + mod_ref[0][2:3] * y


def _swa(sink, sq, local_kv, ctx_kv, yr, xs, mod, w_out, tq):
    B, n, D = xs.shape
    tq = min(tq, n)
    local = local_kv is not None
    if local:
        assert n >= tq + 2 * SWA_WINDOW
    nh = ctx_kv[0].shape[1]
    kv_specs = ([_seq_spec(n, SWA_KV_W)] * 2 if local else []) + [_seq_spec(nh, SWA_KV_W)] * 2
    kv_args = (list(local_kv) if local else []) + list(ctx_kv)
    return pl.pallas_call(
        functools.partial(_swa_kernel, tq=tq, n=n, local=local),
        grid=(B, n // tq),
        in_specs=[_smem(), _tok_spec(tq, SWA_Q_W)] + kv_specs
                 + [_tok_spec(tq, RET_V_W), _tok_spec(tq, D), _mod_spec(mod), _resident(w_out.shape)],
        out_specs=_tok_spec(tq, D),
        out_shape=jax.ShapeDtypeStruct((B, n, D), F32),
        compiler_params=_params(2),
        name="swa_local" if local else "swa_ctx",
    )(sink, sq, *kv_args, yr, xs, mod, w_out)


def _rope_tables(n):
    t = np.arange(n)
    row = (t // GRID_W).astype(np.float32)
    col = (t % GRID_W).astype(np.float32)
    n_freq = 16
    freqs = jnp.asarray(ROPE_THETA, F32) ** (-jnp.arange(n_freq, dtype=F32) / n_freq)
    ang = jnp.concatenate([jnp.asarray(row)[:, None] * freqs, jnp.asarray(col)[:, None] * freqs], axis=-1)
    cos, sin = jnp.cos(ang), jnp.sin(ang)
    cos2 = jnp.tile(jnp.concatenate([cos, cos], axis=-1), (1, 2))
    sin2 = jnp.tile(jnp.concatenate([-sin, sin], axis=-1), (1, 2))
    return cos2, sin2


def _even_weights(ab_in, ab_out, q_norm, kv_norm, wq_b, wkv_b, v_norm, w_s, b_s):
    kv_w = MLA_LORA + MLA_ROPE
    w_in = jnp.concatenate([ab_in[:, :MLA_LORA], ab_in[:, kv_w:kv_w + MLA_LORA], ab_in[:, kv_w + MLA_LORA:],
                            ab_in[:, MLA_LORA:kv_w], jnp.zeros((ab_in.shape[0], LANES - MLA_ROPE), ab_in.dtype)],
                           axis=1).astype(BF16)
    wq = wq_b.reshape(MLA_LORA, MLA_HEADS, MLA_NOPE + MLA_ROPE)
    wqn = wq[:, :, :MLA_NOPE].reshape(MLA_LORA, MLA_HEADS * MLA_NOPE).astype(BF16)
    wqr = jnp.pad(wq[:, :, MLA_NOPE:], ((0, 0), (0, 0), (0, LANES - MLA_ROPE)))
    wqr = wqr.reshape(MLA_LORA, MLA_HEADS * LANES).astype(BF16)
    wkv = wkv_b.reshape(MLA_LORA, MLA_HEADS, MLA_NOPE + MLA_V)
    wkn = wkv[:, :, :MLA_NOPE].reshape(MLA_LORA, MLA_HEADS * MLA_NOPE).astype(BF16)
    wv = wkv[:, :, MLA_NOPE:].reshape(MLA_LORA, MLA_HEADS * MLA_V).astype(BF16)
    bs = jnp.broadcast_to(b_s[:, :, None], (CMLP_GROUPS, CMLP_CHUNK, CMLP_DIM)).astype(F32)
    return dict(w_in=w_in, wqn=wqn, wqr=wqr, wkn=wkn, wv=wv, q_norm=q_norm[None, :], kv_norm=kv_norm[None, :],
                v_norm=v_norm[None, :], ws=w_s.astype(BF16), bs=bs, w_out=ab_out.astype(BF16))


def kernel(x, c, ctx, c_ctx, ada_w, ada_b, norm_mix, norm_ffn, norm_final, ffn_in, ffn_out, ab_in, ab_out,
           mla_q_norm, mla_kv_norm, mla_wq_b, mla_wkv_b, cmlp_v_norm, cmlp_ws, cmlp_bs, cd_in, cd_out,
           ret_decay_fwd, ret_decay_bwd, ret_norm, swa_sink):
    B, n, D = x.shape
    nh = ctx.shape[1]
    depth = ada_w.shape[0]
    tm = 512
    tq = 256

    rows = -(-(B + 1) // 8) * 8
    cond_rows = jnp.concatenate([c, c_ctx[None, :], jnp.zeros((rows - B - 1, D), F32)], axis=0)
    mods = _ada(cond_rows, ada_w, ada_b).reshape(depth, rows, 6, D)
    mod_x = mods[:, :B]
    mod_h = mods[:, B:B + 1]

    cos_x, sin_x = _rope_tables(n)
    cos_h, sin_h = jnp.ones((nh, LANES), F32), jnp.zeros((nh, LANES), F32)
    pe_mask = (jnp.arange(LANES) < MLA_ROPE).astype(F32)[None, :]
    cos_xm, sin_xm = cos_x * pe_mask, sin_x * pe_mask

    h = ctx
    for layer in range(depth):
        last = layer == depth - 1
        j = layer // 2
        g_mix = norm_mix[layer][None, :]
        mx, mh = mod_x[layer], mod_h[layer]
        if layer % 2 == 0:
            w = _even_weights(ab_in[j], ab_out[j], mla_q_norm[j], mla_kv_norm[j], mla_wq_b[j], mla_wkv_b[j],
                              cmlp_v_norm[j], cmlp_ws[j], cmlp_bs[j])
            qx, kx, vx, cmx = _even_in(x, mx, g_mix, w, cos_xm, sin_xm, tm)
            qh, kh, vh, cmh = _even_in(h, mh, g_mix, w, cos_h, sin_h, tm)
            x = _mla_attn(qx, [(kx, vx), (kh, vh)], cmx, x, mx, w["w_out"], tq)
            if not last:
                h = _mla_attn(qh, [(kh, vh)], cmh, h, mh, w["w_out"], tq)
        else:
            w_in = cd_in[j].astype(BF16)
            w_out = cd_out[j].astype(BF16)
            lg = jnp.stack([jax.nn.log_sigmoid(ret_decay_fwd[j].astype(F32)),
                            jax.nn.log_sigmoid(ret_decay_bwd[j].astype(F32))])
            rn = ret_norm[j][None, :]
            rqx, rkx, rvx, rgx, sqx, skx, svx = _odd_in(x, mx, g_mix, w_in, cos_x, sin_x, tm)
            rqh, rkh, rvh, rgh, sqh, skh, svh = _odd_in(h, mh, g_mix, w_in, cos_h, sin_h, tm)
            s0 = jnp.zeros((B, RET_HEADS, RET_QK, RET_V), F32)
            yrh, s_f, s_b = _retention(lg, rqh, rkh, rvh, rgh, s0, s0, rn)
            yrx, _, _ = _retention(lg, rqx, rkx, rvx, rgx, s_f, s_b, rn)
            sink = swa_sink[j].astype(F32)
            x = _swa(sink, sqx, (skx, svx), (skh, svh), yrx, x, mx, w_out, tq)
            if not last:
                h = _swa(sink, sqh, None, (skh, svh), yrh, h, mh, w_out, tq)
        g_ffn = norm_ffn[layer][None, :]
        w1 = ffn_in[layer].astype(BF16)
        w2 = ffn_out[layer].astype(BF16)
        x = _ffn(x, mx, g_ffn, w1, w2, tm, g_final=norm_final[None, :] if last else None)
        if not last:
            h = _ffn(h, mh, g_ffn, w1, w2, tm)
    return x
```

```python
import functools

import numpy as np
import jax
import jax.numpy as jnp
from jax import lax
from jax.experimental import pallas as pl
from jax.experimental.pallas import tpu as pltpu

F32 = jnp.float32
BF16 = jnp.bfloat16

GRID_W = 64
ROPE_THETA = 10000.0
EPS = 1e-6
NEG_INF = -1e30
LOG2E = 1.4426950408889634

LANES = 128

MLA_HEADS = 4
MLA_LORA = 256
MLA_NOPE = 128
MLA_ROPE = 64
MLA_V = 128
MLA_QK_PAD = 256
CMLP_GROUPS = 4
CMLP_DIM = 128
CMLP_CHUNK = 128
CMLP_WIDTH = CMLP_GROUPS * CMLP_DIM
RET_HEADS = 4
RET_QK = 64
RET_V = 128
RET_CHUNK = 128
RET_QK_W = RET_HEADS * RET_QK
RET_V_W = RET_HEADS * RET_V
SWA_Q_HEADS = 8
SWA_KV_HEADS = 2
SWA_HEAD_DIM = 64
SWA_WINDOW = 128
SWA_Q_W = SWA_Q_HEADS * SWA_HEAD_DIM
SWA_KV_W = SWA_KV_HEADS * SWA_HEAD_DIM

VMEM_LIMIT_BYTES = 56 * 2**20

_NT = (((1,), (1,)), ((), ()))
_TN = (((0,), (0,)), ((), ()))


def _dot(a, b):
    return jnp.dot(a, b, preferred_element_type=F32)


def _dot_nt(a, b):
    return lax.dot_general(a, b, _NT, preferred_element_type=F32)


def _dot_tn(a, b):
    return lax.dot_general(a, b, _TN, preferred_element_type=F32)


def _rms(x, g):
    return x * lax.rsqrt(jnp.mean(x * x, axis=-1, keepdims=True) + EPS) * g


def _rms_mod(x, g, shift, scale):
    return _rms(x, g) * (1.0 + scale) + shift


def _rope128(x, cos, sin):
    lane = lax.broadcasted_iota(jnp.int32, x.shape, 1)
    first = (lane & 32) == 0
    swapped = jnp.where(first, pltpu.roll(x, LANES - 32, 1), pltpu.roll(x, 32, 1))
    return x * cos + swapped * sin


def _rope(x, cos, sin):
    cols = [_rope128(x[:, i:i + LANES], cos, sin) for i in range(0, x.shape[1], LANES)]
    return cols[0] if len(cols) == 1 else jnp.concatenate(cols, axis=-1)


def _params(n_grid):
    return pltpu.CompilerParams(dimension_semantics=("parallel",) * n_grid,
                                vmem_limit_bytes=VMEM_LIMIT_BYTES)


def _resident(shape):
    nd = len(shape)
    return pl.BlockSpec(shape, lambda *_: (0,) * nd, pipeline_mode=pl.Buffered(1))


def _smem():
    return pl.BlockSpec(memory_space=pltpu.SMEM)


def _tok_spec(tm, width):
    return pl.BlockSpec((1, tm, width), lambda b, i: (b, i, 0))


def _seq_spec(n, width):
    return pl.BlockSpec((1, n, width), lambda b, i: (b, 0, 0))


def _mod_spec(mod):
    if mod.shape[0] == 1:
        return pl.BlockSpec((1,) + mod.shape[1:], lambda b, i: (0, 0, 0))
    return pl.BlockSpec((1,) + mod.shape[1:], lambda b, i: (b, 0, 0))


def _pos_spec(tm):
    return pl.BlockSpec((tm, LANES), lambda b, i: (i, 0))


def _ada_kernel(c_ref, w_ref, b_ref, o_ref):
    cond = jax.nn.silu(c_ref[...]).astype(BF16)
    o_ref[0] = _dot(cond, w_ref[0].astype(BF16)) + b_ref[0]


def _ada(cond_rows, ada_w, ada_b):
    L, D, N = ada_w.shape
    R = cond_rows.shape[0]
    tn = 1536
    return pl.pallas_call(
        _ada_kernel,
        grid=(L, N // tn),
        in_specs=[pl.BlockSpec((R, D), lambda l, j: (0, 0)),
                  pl.BlockSpec((1, D, tn), lambda l, j: (l, 0, j)),
                  pl.BlockSpec((1, 1, tn), lambda l, j: (l, 0, j))],
        out_specs=pl.BlockSpec((1, R, tn), lambda l, j: (l, 0, j)),
        out_shape=jax.ShapeDtypeStruct((L, R, N), F32),
        compiler_params=_params(2),
        name="ada_mod",
    )(cond_rows, ada_w, ada_b.reshape(L, 1, N))


def _even_in_kernel(x_ref, mod_ref, g_ref, win_ref, wqn_ref, wqr_ref, wkn_ref, wv_ref,
                    qn_ref, kvn_ref, vn_ref, ws_ref, bs_ref, cos_ref, sin_ref,
                    q_out, k_out, v_out, cm_out, *, tm, scale):
    mod = mod_ref[0]
    xn = _rms_mod(x_ref[0], g_ref[...], mod[0:1], mod[1:2]).astype(BF16)
    cos = cos_ref[...]
    sin = sin_ref[...]
    L = MLA_LORA
    kv_lat = _dot(xn, win_ref[:, 0:L])
    q_lat = _dot(xn, win_ref[:, L:2 * L])
    uv_off = 2 * L
    pe_off = uv_off + 2 * CMLP_WIDTH
    kpe = _rope128(_dot(xn, win_ref[:, pe_off:pe_off + LANES]), cos, sin).astype(BF16)
    kvn = _rms(kv_lat, kvn_ref[...]).astype(BF16)
    qn = _rms(q_lat, qn_ref[...]).astype(BF16)
    k_nope = _dot(kvn, wkn_ref[...])
    v_out[0] = _dot(kvn, wv_ref[...]).astype(BF16)
    q_nope = _dot(qn, wqn_ref[...]) * scale
    q_rope = _dot(qn, wqr_ref[...]) * scale
    for h in range(MLA_HEADS):
        src = slice(LANES * h, LANES * (h + 1))
        lo = MLA_QK_PAD * h
        q_out[0, :, lo:lo + LANES] = q_nope[:, src].astype(BF16)
        q_out[0, :, lo + LANES:lo + 2 * LANES] = _rope128(q_rope[:, src], cos, sin).astype(BF16)
        k_out[0, :, lo:lo + LANES] = k_nope[:, src].astype(BF16)
        k_out[0, :, lo + LANES:lo + 2 * LANES] = kpe
    uv = jax.nn.gelu(_dot(xn, win_ref[:, uv_off:uv_off + 2 * CMLP_WIDTH]))
    for g in range(CMLP_GROUPS):
        col = slice(CMLP_DIM * g, CMLP_DIM * (g + 1))
        vg = uv[:, CMLP_WIDTH + CMLP_DIM * g:CMLP_WIDTH + CMLP_DIM * (g + 1)]
        vg = _rms(vg, vn_ref[:, col]).astype(BF16)
        ug = uv[:, col]
        wsg = ws_ref[g]
        bsg = bs_ref[g]
        for c in range(tm // CMLP_CHUNK):
            rows = slice(CMLP_CHUNK * c, CMLP_CHUNK * (c + 1))
            mixed = _dot(wsg, vg[rows]) + bsg
            cm_out[0, rows, col] = (ug[rows] * mixed).astype(BF16)


def _even_in(xs, mod, g, w, cos, sin, tm):
    B, n, D = xs.shape
    tm = min(tm, n)
    qk_w = MLA_HEADS * MLA_QK_PAD
    v_w = MLA_HEADS * MLA_V
    scale = float((MLA_NOPE + MLA_ROPE) ** -0.5 * LOG2E)
    weights = (w["w_in"], w["wqn"], w["wqr"], w["wkn"], w["wv"], w["q_norm"], w["kv_norm"],
               w["v_norm"], w["ws"], w["bs"])
    return pl.pallas_call(
        functools.partial(_even_in_kernel, tm=tm, scale=scale),
        grid=(B, n // tm),
        in_specs=[_tok_spec(tm, D), _mod_spec(mod), _resident(g.shape)]
                 + [_resident(a.shape) for a in weights] + [_pos_spec(tm), _pos_spec(tm)],
        out_specs=[_tok_spec(tm, qk_w), _tok_spec(tm, qk_w), _tok_spec(tm, v_w), _tok_spec(tm, CMLP_WIDTH)],
        out_shape=[jax.ShapeDtypeStruct((B, n, qk_w), BF16), jax.ShapeDtypeStruct((B, n, qk_w), BF16),
                   jax.ShapeDtypeStruct((B, n, v_w), BF16), jax.ShapeDtypeStruct((B, n, CMLP_WIDTH), BF16)],
        compiler_params=_params(2),
        name="even_in",
    )(xs, mod, g, *weights, cos, sin)


def _mla_attn_kernel(*refs, n_pieces):
    q_ref = refs[0]
    kv_refs = refs[1:1 + 2 * n_pieces]
    cm_ref, x_ref, mod_ref, wout_ref, o_ref = refs[1 + 2 * n_pieces:]
    q = q_ref[0]
    heads = []
    for h in range(MLA_HEADS):
        qk = slice(MLA_QK_PAD * h, MLA_QK_PAD * (h + 1))
        vc = slice(MLA_V * h, MLA_V * (h + 1))
        qh = q[:, qk]
        scores = [_dot_nt(qh, kv_refs[2 * p][0, :, qk]) for p in range(n_pieces)]
        m = functools.reduce(jnp.maximum, [jnp.max(s, axis=-1, keepdims=True) for s in scores])
        probs = [jnp.exp2(s - m) for s in scores]
        denom = sum(jnp.sum(p, axis=-1, keepdims=True) for p in probs)
        o = sum(_dot(probs[p].astype(BF16), kv_refs[2 * p + 1][0, :, vc]) for p in range(n_pieces))
        heads.append(o / denom)
    o = jnp.concatenate(heads, axis=-1).astype(BF16)
    ow = MLA_HEADS * MLA_V
    y = _dot(o, wout_ref[0:ow, :]) + _dot(cm_ref[0], wout_ref[ow:ow + CMLP_WIDTH, :])
    o_ref[0] = x_ref[0] + mod_ref[0][2:3] * y


def _mla_attn(q, pieces, cm, xs, mod, w_out, tq):
    B, n, D = xs.shape
    tq = min(tq, n)
    kv_specs, kv_args = [], []
    for k, v in pieces:
        kv_specs += [_seq_spec(k.shape[1], k.shape[2]), _seq_spec(v.shape[1], v.shape[2])]
        kv_args += [k, v]
    return pl.pallas_call(
        functools.partial(_mla_attn_kernel, n_pieces=len(pieces)),
        grid=(B, n // tq),
        in_specs=[_tok_spec(tq, q.shape[2])] + kv_specs
                 + [_tok_spec(tq, cm.shape[2]), _tok_spec(tq, D), _mod_spec(mod), _resident(w_out.shape)],
        out_specs=_tok_spec(tq, D),
        out_shape=jax.ShapeDtypeStruct((B, n, D), F32),
        compiler_params=_params(2),
        name="mla_attn",
    )(q, *kv_args, cm, xs, mod, w_out)


def _ffn_kernel(*refs, hidden, th, final):
    if final:
        x_ref, mod_ref, g_ref, win_ref, wout_ref, gf_ref, o_ref = refs
    else:
        x_ref, mod_ref, g_ref, win_ref, wout_ref, o_ref = refs
    x = x_ref[0]
    mod = mod_ref[0]
    xn = _rms_mod(x, g_ref[...], mod[3:4], mod[4:5]).astype(BF16)
    acc = None
    for c in range(hidden // th):
        a = _dot(xn, win_ref[:, c * th:(c + 1) * th])
        b = _dot(xn, win_ref[:, hidden + c * th:hidden + (c + 1) * th])
        d = _dot((jax.nn.silu(a) * b).astype(BF16), wout_ref[c * th:(c + 1) * th, :])
        acc = d if acc is None else acc + d
    out = x + mod[5:6] * acc
    if final:
        out = _rms(out, gf_ref[...])
    o_ref[0] = out


def _ffn(xs, mod, g, w_in, w_out, tm, g_final=None):
    B, n, D = xs.shape
    tm = min(tm, n)
    hidden = w_out.shape[0]
    th = hidden
    final = g_final is not None
    extra = [g_final] if final else []
    return pl.pallas_call(
        functools.partial(_ffn_kernel, hidden=hidden, th=th, final=final),
        grid=(B, n // tm),
        in_specs=[_tok_spec(tm, D), _mod_spec(mod), _resident(g.shape), _resident(w_in.shape),
                  _resident(w_out.shape)] + [_resident(a.shape) for a in extra],
        out_specs=_tok_spec(tm, D),
        out_shape=jax.ShapeDtypeStruct((B, n, D), F32),
        compiler_params=_params(2),
        name="ffn",
    )(xs, mod, g, w_in, w_out, *extra)


def _odd_in_kernel(x_ref, mod_ref, g_ref, w_ref, cos_ref, sin_ref,
                   rq_out, rk_out, rv_out, rg_out, sq_out, sk_out, sv_out):
    mod = mod_ref[0]
    xn = _rms_mod(x_ref[0], g_ref[...], mod[0:1], mod[1:2]).astype(BF16)
    cos = cos_ref[...]
    sin = sin_ref[...]

    def proj(lo, width):
        return _dot(xn, w_ref[:, lo:lo + width])

    o = 0
    rk_out[0] = (_rope(proj(o, RET_QK_W), cos, sin) * (RET_QK ** -0.5)).astype(BF16)
    o += RET_QK_W
    rv_out[0] = proj(o, RET_V_W).astype(BF16)
    o += RET_V_W
    sk_out[0] = _rope(proj(o, SWA_KV_W), cos, sin).astype(BF16)
    o += SWA_KV_W
    sv_out[0] = proj(o, SWA_KV_W).astype(BF16)
    o += SWA_KV_W
    rq_out[0] = _rope(proj(o, RET_QK_W), cos, sin).astype(BF16)
    o += RET_QK_W
    rg_out[0] = proj(o, RET_V_W)
    o += RET_V_W
    sq_out[0] = (_rope(proj(o, SWA_Q_W), cos, sin) * (SWA_HEAD_DIM ** -0.5 * LOG2E)).astype(BF16)


def _odd_in(xs, mod, g, w_in, cos, sin, tm):
    B, n, D = xs.shape
    tm = min(tm, n)
    widths = [(RET_QK_W, BF16), (RET_QK_W, BF16), (RET_V_W, BF16), (RET_V_W, F32),
              (SWA_Q_W, BF16), (SWA_KV_W, BF16), (SWA_KV_W, BF16)]
    return pl.pallas_call(
        _odd_in_kernel,
        grid=(B, n // tm),
        in_specs=[_tok_spec(tm, D), _mod_spec(mod), _resident(g.shape), _resident(w_in.shape),
                  _pos_spec(tm), _pos_spec(tm)],
        out_specs=[_tok_spec(tm, w) for w, _ in widths],
        out_shape=[jax.ShapeDtypeStruct((B, n, w), dt) for w, dt in widths],
        compiler_params=_params(2),
        name="odd_in",
    )(xs, mod, g, w_in, cos, sin)


def _ret_kernel(lg_ref, rq_ref, rk_ref, rv_ref, rg_ref, sf0_ref, sb0_ref, rn_ref,
                yr_ref, sfo_ref, sbo_ref, mask_scr, dq_scr, dk_scr, kv_scr, s_scr, *, nc):
    C = RET_CHUNK
    HQ = RET_QK
    row = lax.broadcasted_iota(jnp.int32, (C, C), 0)
    col = lax.broadcasted_iota(jnp.int32, (C, C), 1)
    diff = (row - col).astype(F32)
    pos = row.astype(F32)
    lo = col < HQ
    c_dec = []
    for h in range(RET_HEADS):
        lf = lg_ref[0, h]
        lb = lg_ref[1, h]
        mask_scr[h] = 0.5 * (jnp.where(diff >= 0, jnp.exp(lf * jnp.maximum(diff, 0.0)), 0.0)
                             + jnp.where(diff <= 0, jnp.exp(lb * jnp.maximum(-diff, 0.0)), 0.0))
        dq_scr[h] = jnp.where(lo, jnp.exp((pos + 1.0) * lf), jnp.exp((C - pos) * lb))
        dk_scr[h] = jnp.where(lo, jnp.exp((C - 1.0 - pos) * lf), jnp.exp(pos * lb))
        c_dec.append((jnp.exp(jnp.full((1, 1), C, F32) * lf), jnp.exp(jnp.full((1, 1), C, F32) * lb)))

    def doubled(ref, rows, p):
        ab = ref[0, rows, LANES * p:LANES * (p + 1)].astype(F32)
        ba = pltpu.roll(ab, HQ, 1)
        return jnp.where(lo, ab, ba), jnp.where(lo, ba, ab)

    def chunk_kv(c, carry):
        rows = pl.ds(pl.multiple_of(c * C, C), C)
        for p in range(RET_HEADS // 2):
            for e, k2 in enumerate(doubled(rk_ref, rows, p)):
                h = 2 * p + e
                kd = (k2 * dk_scr[h]).astype(BF16)
                kv_scr[c, h] = _dot_tn(kd, rv_ref[0, rows, RET_V * h:RET_V * (h + 1)])
        return carry

    lax.fori_loop(0, nc, chunk_kv, 0, unroll=min(4, nc))

    for h in range(RET_HEADS):
        c_dec_f, c_dec_b = c_dec[h]

        def fwd(c, s, h=h, c_dec_f=c_dec_f):
            s_scr[c, h, 0:HQ, :] = s.astype(BF16)
            return s * c_dec_f + kv_scr[c, h, 0:HQ, :]

        sfo_ref[0, h] = lax.fori_loop(0, nc, fwd, sf0_ref[0, h])

        def bwd(i, s, h=h, c_dec_b=c_dec_b):
            c = nc - 1 - i
            s_scr[c, h, HQ:2 * HQ, :] = s.astype(BF16)
            return s * c_dec_b + kv_scr[c, h, HQ:2 * HQ, :]

        sbo_ref[0, h] = lax.fori_loop(0, nc, bwd, sb0_ref[0, h])

    def chunk_out(c, carry):
        rows = pl.ds(pl.multiple_of(c * C, C), C)
        for p in range(RET_HEADS // 2):
            q2s = doubled(rq_ref, rows, p)
            k2s = doubled(rk_ref, rows, p)
            for e in range(2):
                h = 2 * p + e
                vc = slice(RET_V * h, RET_V * (h + 1))
                q2 = q2s[e]
                att = (_dot_nt(q2.astype(BF16), k2s[e].astype(BF16)) * mask_scr[h]).astype(BF16)
                y = _dot(att, rv_ref[0, rows, vc]) + _dot((q2 * dq_scr[h]).astype(BF16), s_scr[c, h])
                y = _rms(y, rn_ref[:, vc]) * jax.nn.silu(rg_ref[0, rows, vc])
                yr_ref[0, rows, vc] = y.astype(BF16)
        return carry

    lax.fori_loop(0, nc, chunk_out, 0, unroll=min(4, nc))


def _retention(lg, rq, rk, rv, rg, sf0, sb0, ret_norm):
    B, n, _ = rq.shape
    nc = n // RET_CHUNK
    st_shape = (B, RET_HEADS, RET_QK, RET_V)
    st_spec = pl.BlockSpec((1, RET_HEADS, RET_QK, RET_V), lambda b: (b, 0, 0, 0))

    def seq(width):
        return pl.BlockSpec((1, n, width), lambda b: (b, 0, 0))

    return pl.pallas_call(
        functools.partial(_ret_kernel, nc=nc),
        grid=(B,),
        in_specs=[_smem(), seq(RET_QK_W), seq(RET_QK_W), seq(RET_V_W), seq(RET_V_W), st_spec, st_spec,
                  _resident(ret_norm.shape)],
        out_specs=[seq(RET_V_W), st_spec, st_spec],
        out_shape=[jax.ShapeDtypeStruct((B, n, RET_V_W), BF16), jax.ShapeDtypeStruct(st_shape, F32),
                   jax.ShapeDtypeStruct(st_shape, F32)],
        scratch_shapes=[pltpu.VMEM((RET_HEADS, RET_CHUNK, RET_CHUNK), F32),
                        pltpu.VMEM((RET_HEADS, RET_CHUNK, LANES), F32),
                        pltpu.VMEM((RET_HEADS, RET_CHUNK, LANES), F32),
                        pltpu.VMEM((nc, RET_HEADS, 2 * RET_QK, RET_V), F32),
                        pltpu.VMEM((nc, RET_HEADS, 2 * RET_QK, RET_V), BF16)],
        compiler_params=_params(1),
        name="retention",
    )(lg, rq, rk, rv, rg, sf0, sb0, ret_norm)


def _swa_kernel(*refs, tq, n, local):
    if local:
        sink_ref, sq_ref, skx_ref, svx_ref, skh_ref, svh_ref, yr_ref, x_ref, mod_ref, wout_ref, o_ref = refs
    else:
        sink_ref, sq_ref, skh_ref, svh_ref, yr_ref, x_ref, mod_ref, wout_ref, o_ref = refs
    HD = SWA_HEAD_DIM
    groups = SWA_Q_HEADS // SWA_KV_HEADS
    gq = groups * tq
    q_t = sq_ref[0].astype(F32).T.astype(BF16)
    kh = skh_ref[0]
    vh_t = svh_ref[0].astype(F32).T.astype(BF16)
    if local:
        i = pl.program_id(1)
        win = tq + 2 * SWA_WINDOW
        start = pl.multiple_of(jnp.clip(i * tq - SWA_WINDOW, 0, n - win), SWA_WINDOW)
        kx = skx_ref[0, pl.ds(start, win), :]
        vx_t = svx_ref[0, pl.ds(start, win), :].astype(F32).T.astype(BF16)
        rel = (lax.broadcasted_iota(jnp.int32, (win, tq), 0) - lax.broadcasted_iota(jnp.int32, (win, tq), 1)
               + (start - i * tq))
        cap = jnp.where(jnp.abs(rel) <= SWA_WINDOW, jnp.inf, NEG_INF).astype(F32)
        cap = jnp.concatenate([cap] * groups, axis=1)
    heads = []
    for j in range(SWA_KV_HEADS):
        kvc = slice(HD * j, HD * (j + 1))
        h0 = groups * j
        qj_t = jnp.concatenate([q_t[HD * (h0 + g):HD * (h0 + g + 1), :] for g in range(groups)], axis=1)
        sink = jnp.concatenate([jnp.full((1, tq), sink_ref[h0 + g] * LOG2E, F32) for g in range(groups)], axis=1)
        s_h = _dot(kh[:, kvc], qj_t)
        m = jnp.maximum(jnp.max(s_h, axis=0, keepdims=True), sink)
        if local:
            s_x = jnp.minimum(_dot(kx[:, kvc], qj_t), cap)
            m = jnp.maximum(m, jnp.max(s_x, axis=0, keepdims=True))
        p_h = jnp.exp2(s_h - m)
        denom = jnp.sum(p_h, axis=0, keepdims=True) + jnp.exp2(sink - m)
        o_t = _dot(vh_t[kvc, :], p_h.astype(BF16))
        if local:
            p_x = jnp.exp2(s_x - m)
            denom = denom + jnp.sum(p_x, axis=0, keepdims=True)
            o_t = o_t + _dot(vx_t[kvc, :], p_x.astype(BF16))
        o_t = o_t / denom
        heads += [o_t[:, g * tq:(g + 1) * tq] for g in range(groups)]
    o = jnp.concatenate(heads, axis=0).T.astype(BF16)
    y = _dot(yr_ref[0], wout_ref[0:RET_V_W, :]) + _dot(o, wout_ref[RET_V_W:RET_V_W + SWA_Q_W, :])
    o_ref[0] = x_ref[0] + mod_ref[0][2:3] * y


def _swa(sink, sq, local_kv, ctx_kv, yr, xs, mod, w_out, tq):
    B, n, D = xs.shape
    tq = min(tq, n)
    local = local_kv is not None
    if local:
        assert n >= tq + 2 * SWA_WINDOW
    nh = ctx_kv[0].shape[1]
    kv_specs = ([_seq_spec(n, SWA_KV_W)] * 2 if local else []) + [_seq_spec(nh, SWA_KV_W)] * 2
    kv_args = (list(local_kv) if local else []) + list(ctx_kv)
    return pl.pallas_call(
        functools.partial(_swa_kernel, tq=tq, n=n, local=local),
        grid=(B, n // tq),
        in_specs=[_smem(), _tok_spec(tq, SWA_Q_W)] + kv_specs
                 + [_tok_spec(tq, RET_V_W), _tok_spec(tq, D), _mod_spec(mod), _resident(w_out.shape)],
        out_specs=_tok_spec(tq, D),
        out_shape=jax.ShapeDtypeStruct((B, n, D), F32),
        compiler_params=_params(2),
        name="swa_local" if local else "swa_ctx",
    )(sink, sq, *kv_args, yr, xs, mod, w_out)


def _rope_tables(n):
    t = np.arange(n)
    row = (t // GRID_W).astype(np.float32)
    col = (t % GRID_W).astype(np.float32)
    n_freq = 16
    freqs = jnp.asarray(ROPE_THETA, F32) ** (-jnp.arange(n_freq, dtype=F32) / n_freq)
    ang = jnp.concatenate([jnp.asarray(row)[:, None] * freqs, jnp.asarray(col)[:, None] * freqs], axis=-1)
    cos, sin = jnp.cos(ang), jnp.sin(ang)
    cos2 = jnp.tile(jnp.concatenate([cos, cos], axis=-1), (1, 2))
    sin2 = jnp.tile(jnp.concatenate([-sin, sin], axis=-1), (1, 2))
    return cos2, sin2


def _even_weights(ab_in, ab_out, q_norm, kv_norm, wq_b, wkv_b, v_norm, w_s, b_s):
    kv_w = MLA_LORA + MLA_ROPE
    w_in = jnp.concatenate([ab_in[:, :MLA_LORA], ab_in[:, kv_w:kv_w + MLA_LORA], ab_in[:, kv_w + MLA_LORA:],
                            ab_in[:, MLA_LORA:kv_w], jnp.zeros((ab_in.shape[0], LANES - MLA_ROPE), ab_in.dtype)],
                           axis=1).astype(BF16)
    wq = wq_b.reshape(MLA_LORA, MLA_HEADS, MLA_NOPE + MLA_ROPE)
    wqn = wq[:, :, :MLA_NOPE].reshape(MLA_LORA, MLA_HEADS * MLA_NOPE).astype(BF16)
    wqr = jnp.pad(wq[:, :, MLA_NOPE:], ((0, 0), (0, 0), (0, LANES - MLA_ROPE)))
    wqr = wqr.reshape(MLA_LORA, MLA_HEADS * LANES).astype(BF16)
    wkv = wkv_b.reshape(MLA_LORA, MLA_HEADS, MLA_NOPE + MLA_V)
    wkn = wkv[:, :, :MLA_NOPE].reshape(MLA_LORA, MLA_HEADS * MLA_NOPE).astype(BF16)
    wv = wkv[:, :, MLA_NOPE:].reshape(MLA_LORA, MLA_HEADS * MLA_V).astype(BF16)
    bs = jnp.broadcast_to(b_s[:, :, None], (CMLP_GROUPS, CMLP_CHUNK, CMLP_DIM)).astype(F32)
    return dict(w_in=w_in, wqn=wqn, wqr=wqr, wkn=wkn, wv=wv, q_norm=q_norm[None, :], kv_norm=kv_norm[None, :],
                v_norm=v_norm[None, :], ws=w_s.astype(BF16), bs=bs, w_out=ab_out.astype(BF16))


def kernel(x, c, ctx, c_ctx, ada_w, ada_b, norm_mix, norm_ffn, norm_final, ffn_in, ffn_out, ab_in, ab_out,
           mla_q_norm, mla_kv_norm, mla_wq_b, mla_wkv_b, cmlp_v_norm, cmlp_ws, cmlp_bs, cd_in, cd_out,
           ret_decay_fwd, ret_decay_bwd, ret_norm, swa_sink):
    B, n, D = x.shape
    nh = ctx.shape[1]
    depth = ada_w.shape[0]
    tm = 512
    tq = 256

    rows = -(-(B + 1) // 8) * 8
    cond_rows = jnp.concatenate([c, c_ctx[None, :], jnp.zeros((rows - B - 1, D), F32)], axis=0)
    mods = _ada(cond_rows, ada_w, ada_b).reshape(depth, rows, 6, D)
    mod_x = mods[:, :B]
    mod_h = mods[:, B:B + 1]

    cos_x, sin_x = _rope_tables(n)
    cos_h, sin_h = jnp.ones((nh, LANES), F32), jnp.zeros((nh, LANES), F32)
    pe_mask = (jnp.arange(LANES) < MLA_ROPE).astype(F32)[None, :]
    cos_xm, sin_xm = cos_x * pe_mask, sin_x * pe_mask

    h = ctx
    for layer in range(depth):
        last = layer == depth - 1
        j = layer // 2
        g_mix = norm_mix[layer][None, :]
        mx, mh = mod_x[layer], mod_h[layer]
        if layer % 2 == 0:
            w = _even_weights(ab_in[j], ab_out[j], mla_q_norm[j], mla_kv_norm[j], mla_wq_b[j], mla_wkv_b[j],
                              cmlp_v_norm[j], cmlp_ws[j], cmlp_bs[j])
            qx, kx, vx, cmx = _even_in(x, mx, g_mix, w, cos_xm, sin_xm, tm)
            qh, kh, vh, cmh = _even_in(h, mh, g_mix, w, cos_h, sin_h, tm)
            x = _mla_attn(qx, [(kx, vx), (kh, vh)], cmx, x, mx, w["w_out"], tq)
            if not last:
                h = _mla_attn(qh, [(kh, vh)], cmh, h, mh, w["w_out"], tq)
        else:
            w_in = cd_in[j].astype(BF16)
            w_out = cd_out[j].astype(BF16)
            lg = jnp.stack([jax.nn.log_sigmoid(ret_decay_fwd[j].astype(F32)),
                            jax.nn.log_sigmoid(ret_decay_bwd[j].astype(F32))])
            rn = ret_norm[j][None, :]
            rqx, rkx, rvx, rgx, sqx, skx, svx = _odd_in(x, mx, g_mix, w_in, cos_x, sin_x, tm)
            rqh, rkh, rvh, rgh, sqh, skh, svh = _odd_in(h, mh, g_mix, w_in, cos_h, sin_h, tm)
            s0 = jnp.zeros((B, RET_HEADS, RET_QK, RET_V), F32)
            yrh, s_f, s_b = _retention(lg, rqh, rkh, rvh, rgh, s0, s0, rn)
            yrx, _, _ = _retention(lg, rqx, rkx, rvx, rgx, s_f, s_b, rn)
            sink = swa_sink[j].astype(F32)
            x = _swa(sink, sqx, (skx, svx), (skh, svh), yrx, x, mx, w_out, tq)
            if not last:
                h = _swa(sink, sqh, None, (skh, svh), yrh, h, mh, w_out, tq)
        g_ffn = norm_ffn[layer][None, :]
        w1 = ffn_in[layer].astype(BF16)
        w2 = ffn_out[layer].astype(BF16)
        x = _ffn(x, mx, g_ffn, w1, w2, tm, g_final=norm_final[None, :] if last else None)
        if not last:
            h = _ffn(h, mh, g_ffn, w1, w2, tm)
    return x
```

```python
import functools

import numpy as np
import jax
import jax.numpy as jnp
from jax import lax
from jax.experimental import pallas as pl
from jax.experimental.pallas import tpu as pltpu

F32 = jnp.float32
BF16 = jnp.bfloat16

GRID_W = 64
ROPE_THETA = 10000.0
EPS = 1e-6
NEG_INF = -1e30
LOG2E = 1.4426950408889634

LANES = 128

MLA_HEADS = 4
MLA_LORA = 256
MLA_NOPE = 128
MLA_ROPE = 64
MLA_V = 128
MLA_QK_PAD = 256
CMLP_GROUPS = 4
CMLP_DIM = 128
CMLP_CHUNK = 128
CMLP_WIDTH = CMLP_GROUPS * CMLP_DIM
RET_HEADS = 4
RET_QK = 64
RET_V = 128
RET_CHUNK = 128
RET_QK_W = RET_HEADS * RET_QK
RET_V_W = RET_HEADS * RET_V
SWA_Q_HEADS = 8
SWA_KV_HEADS = 2
SWA_HEAD_DIM = 64
SWA_WINDOW = 128
SWA_Q_W = SWA_Q_HEADS * SWA_HEAD_DIM
SWA_KV_W = SWA_KV_HEADS * SWA_HEAD_DIM

VMEM_LIMIT_BYTES = 56 * 2**20

_NT = (((1,), (1,)), ((), ()))
_TN = (((0,), (0,)), ((), ()))


def _dot(a, b):
    return jnp.dot(a, b, preferred_element_type=F32)


def _dot_nt(a, b):
    return lax.dot_general(a, b, _NT, preferred_element_type=F32)


def _dot_tn(a, b):
    return lax.dot_general(a, b, _TN, preferred_element_type=F32)


def _rms(x, g):
    return x * lax.rsqrt(jnp.mean(x * x, axis=-1, keepdims=True) + EPS) * g


def _rms_mod(x, g, shift, scale):
    return x * lax.rsqrt(jnp.mean(x * x, axis=-1, keepdims=True) + EPS) * (g * (1.0 + scale)) + shift


def _rope128(x, cos, sin):
    lane = lax.broadcasted_iota(jnp.int32, x.shape, 1)
    first = (lane & 32) == 0
    swapped = jnp.where(first, pltpu.roll(x, LANES - 32, 1), pltpu.roll(x, 32, 1))
    return x * cos + swapped * sin


def _rope(x, cos, sin):
    cols = [_rope128(x[:, i:i + LANES], cos, sin) for i in range(0, x.shape[1], LANES)]
    return cols[0] if len(cols) == 1 else jnp.concatenate(cols, axis=-1)


def _params(n_grid):
    return pltpu.CompilerParams(dimension_semantics=("parallel",) * n_grid,
                                vmem_limit_bytes=VMEM_LIMIT_BYTES)


def _resident(shape):
    nd = len(shape)
    return pl.BlockSpec(shape, lambda *_: (0,) * nd, pipeline_mode=pl.Buffered(1))


def _smem():
    return pl.BlockSpec(memory_space=pltpu.SMEM)


def _tok_spec(tm, width):
    return pl.BlockSpec((1, tm, width), lambda b, i: (b, i, 0))


def _seq_spec(n, width):
    return pl.BlockSpec((1, n, width), lambda b, i: (b, 0, 0))


def _mod_spec(mod):
    if mod.shape[0] == 1:
        return pl.BlockSpec((1,) + mod.shape[1:], lambda b, i: (0, 0, 0))
    return pl.BlockSpec((1,) + mod.shape[1:], lambda b, i: (b, 0, 0))


def _pos_spec(tm, table):
    nblk = table.shape[0] // tm
    return pl.BlockSpec((tm, LANES), lambda b, i: (i % nblk, 0))


def _ada_kernel(c_ref, w_ref, b_ref, o_ref):
    cond = jax.nn.silu(c_ref[...]).astype(BF16)
    o_ref[0] = _dot(cond, w_ref[0].astype(BF16)) + b_ref[0]


def _ada(cond_rows, ada_w, ada_b):
    L, D, N = ada_w.shape
    R = cond_rows.shape[0]
    tn = 1536
    return pl.pallas_call(
        _ada_kernel,
        grid=(L, N // tn),
        in_specs=[pl.BlockSpec((R, D), lambda l, j: (0, 0)),
                  pl.BlockSpec((1, D, tn), lambda l, j: (l, 0, j)),
                  pl.BlockSpec((1, 1, tn), lambda l, j: (l, 0, j))],
        out_specs=pl.BlockSpec((1, R, tn), lambda l, j: (l, 0, j)),
        out_shape=jax.ShapeDtypeStruct((L, R, N), F32),
        compiler_params=_params(2),
        name="ada_mod",
    )(cond_rows, ada_w, ada_b.reshape(L, 1, N))


def _even_in_kernel(x_ref, mod_ref, g_ref, win_ref, wqn_ref, wqr_ref, wkn_ref, wv_ref,
                    qn_ref, kvn_ref, vn_ref, ws_ref, bs_ref, cos_ref, sin_ref,
                    q_out, k_out, v_out, cm_out, *, tm, scale):
    mod = mod_ref[0]
    xn = _rms_mod(x_ref[0], g_ref[...], mod[0:1], mod[1:2]).astype(BF16)
    cos = cos_ref[...]
    sin = sin_ref[...]
    L = MLA_LORA
    kv_lat = _dot(xn, win_ref[:, 0:L])
    q_lat = _dot(xn, win_ref[:, L:2 * L])
    uv_off = 2 * L
    pe_off = uv_off + 2 * CMLP_WIDTH
    kpe = _rope128(_dot(xn, win_ref[:, pe_off:pe_off + LANES]), cos, sin).astype(BF16)
    kvn = _rms(kv_lat, kvn_ref[...]).astype(BF16)
    qn = _rms(q_lat, qn_ref[...]).astype(BF16)
    k_nope = _dot(kvn, wkn_ref[...])
    v_out[0] = _dot(kvn, wv_ref[...]).astype(BF16)
    q_nope = _dot(qn, wqn_ref[...]) * scale
    q_rope = _dot(qn, wqr_ref[...]) * scale
    for h in range(MLA_HEADS):
        src = slice(LANES * h, LANES * (h + 1))
        lo = MLA_QK_PAD * h
        q_out[0, :, lo:lo + LANES] = q_nope[:, src].astype(BF16)
        q_out[0, :, lo + LANES:lo + 2 * LANES] = _rope128(q_rope[:, src], cos, sin).astype(BF16)
        k_out[0, :, lo:lo + LANES] = k_nope[:, src].astype(BF16)
        k_out[0, :, lo + LANES:lo + 2 * LANES] = kpe
    uv = jax.nn.gelu(_dot(xn, win_ref[:, uv_off:uv_off + 2 * CMLP_WIDTH]))
    for g in range(CMLP_GROUPS):
        col = slice(CMLP_DIM * g, CMLP_DIM * (g + 1))
        vg = uv[:, CMLP_WIDTH + CMLP_DIM * g:CMLP_WIDTH + CMLP_DIM * (g + 1)]
        vg = _rms(vg, vn_ref[:, col]).astype(BF16)
        ug = uv[:, col]
        wsg = ws_ref[g]
        bsg = bs_ref[g]
        for c in range(tm // CMLP_CHUNK):
            rows = slice(CMLP_CHUNK * c, CMLP_CHUNK * (c + 1))
            mixed = _dot(wsg, vg[rows]) + bsg
            cm_out[0, rows, col] = (ug[rows] * mixed).astype(BF16)


def _even_in(xs, mod, g, w, cos, sin, tm):
    B, n, D = xs.shape
    tm = min(tm, n)
    qk_w = MLA_HEADS * MLA_QK_PAD
    v_w = MLA_HEADS * MLA_V
    scale = float((MLA_NOPE + MLA_ROPE) ** -0.5 * LOG2E)
    weights = (w["w_in"], w["wqn"], w["wqr"], w["wkn"], w["wv"], w["q_norm"], w["kv_norm"],
               w["v_norm"], w["ws"], w["bs"])
    return pl.pallas_call(
        functools.partial(_even_in_kernel, tm=tm, scale=scale),
        grid=(B, n // tm),
        in_specs=[_tok_spec(tm, D), _mod_spec(mod), _resident(g.shape)]
                 + [_resident(a.shape) for a in weights] + [_pos_spec(tm, cos), _pos_spec(tm, sin)],
        out_specs=[_tok_spec(tm, qk_w), _tok_spec(tm, qk_w), _tok_spec(tm, v_w), _tok_spec(tm, CMLP_WIDTH)],
        out_shape=[jax.ShapeDtypeStruct((B, n, qk_w), BF16), jax.ShapeDtypeStruct((B, n, qk_w), BF16),
                   jax.ShapeDtypeStruct((B, n, v_w), BF16), jax.ShapeDtypeStruct((B, n, CMLP_WIDTH), BF16)],
        compiler_params=_params(2),
        name="even_in",
    )(xs, mod, g, *weights, cos, sin)


def _mla_attn_kernel(*refs, n_pieces):
    q_ref = refs[0]
    kv_refs = refs[1:1 + 2 * n_pieces]
    cm_ref, x_ref, mod_ref, wout_ref, o_ref = refs[1 + 2 * n_pieces:]
    q = q_ref[0]
    ow = MLA_HEADS * MLA_V
    y_cm = _dot(cm_ref[0], wout_ref[ow:ow + CMLP_WIDTH, :])

    def score(h):
        qk = slice(MLA_QK_PAD * h, MLA_QK_PAD * (h + 1))
        return [_dot_nt(q[:, qk], kv_refs[2 * p][0, :, qk]) for p in range(n_pieces)]

    heads = []
    nxt = score(0)
    for h in range(MLA_HEADS):
        scores = nxt
        if h + 1 < MLA_HEADS:
            nxt = score(h + 1)
        vc = slice(MLA_V * h, MLA_V * (h + 1))
        m = functools.reduce(jnp.maximum, [jnp.max(s, axis=-1, keepdims=True) for s in scores])
        probs = [jnp.exp2(s - m) for s in scores]
        denom = sum(jnp.sum(p, axis=-1, keepdims=True) for p in probs)
        o = sum(_dot(probs[p].astype(BF16), kv_refs[2 * p + 1][0, :, vc]) for p in range(n_pieces))
        heads.append(o / denom)
    o = jnp.concatenate(heads, axis=-1).astype(BF16)
    y = _dot(o, wout_ref[0:ow, :]) + y_cm
    o_ref[0] = x_ref[0] + mod_ref[0][2:3] * y


def _mla_attn(q, pieces, cm, xs, mod, w_out, tq):
    B, n, D = xs.shape
    tq = min(tq, n)
    kv_specs, kv_args = [], []
    for k, v in pieces:
        kv_specs += [_seq_spec(k.shape[1], k.shape[2]), _seq_spec(v.shape[1], v.shape[2])]
        kv_args += [k, v]
    return pl.pallas_call(
        functools.partial(_mla_attn_kernel, n_pieces=len(pieces)),
        grid=(B, n // tq),
        in_specs=[_tok_spec(tq, q.shape[2])] + kv_specs
                 + [_tok_spec(tq, cm.shape[2]), _tok_spec(tq, D), _mod_spec(mod), _resident(w_out.shape)],
        out_specs=_tok_spec(tq, D),
        out_shape=jax.ShapeDtypeStruct((B, n, D), F32),
        compiler_params=_params(2),
        name="mla_attn",
    )(q, *kv_args, cm, xs, mod, w_out)


def _ffn_kernel(*refs, hidden, th, final):
    if final:
        x_ref, mod_ref, g_ref, win_ref, wout_ref, gf_ref, o_ref = refs
    else:
        x_ref, mod_ref, g_ref, win_ref, wout_ref, o_ref = refs
    x = x_ref[0]
    mod = mod_ref[0]
    xn = _rms_mod(x, g_ref[...], mod[3:4], mod[4:5]).astype(BF16)
    acc = None
    for c in range(hidden // th):
        a = _dot(xn, win_ref[:, c * th:(c + 1) * th])
        b = _dot(xn, win_ref[:, hidden + c * th:hidden + (c + 1) * th])
        d = _dot((jax.nn.silu(a) * b).astype(BF16), wout_ref[c * th:(c + 1) * th, :])
        acc = d if acc is None else acc + d
    out = x + mod[5:6] * acc
    if final:
        out = _rms(out, gf_ref[...])
    o_ref[0] = out


def _ffn(xs, mod, g, w_in, w_out, tm, g_final=None):
    B, n, D = xs.shape
    tm = min(tm, n)
    hidden = w_out.shape[0]
    th = hidden
    final = g_final is not None
    extra = [g_final] if final else []
    return pl.pallas_call(
        functools.partial(_ffn_kernel, hidden=hidden, th=th, final=final),
        grid=(B, n // tm),
        in_specs=[_tok_spec(tm, D), _mod_spec(mod), _resident(g.shape), _resident(w_in.shape),
                  _resident(w_out.shape)] + [_resident(a.shape) for a in extra],
        out_specs=_tok_spec(tm, D),
        out_shape=jax.ShapeDtypeStruct((B, n, D), F32),
        compiler_params=_params(2),
        name="ffn",
    )(xs, mod, g, w_in, w_out, *extra)


def _odd_in_kernel(x_ref, mod_ref, g_ref, w_ref, cos_ref, sin_ref,
                   rq_out, rk_out, rv_out, rg_out, sq_out, sk_out, sv_out):
    mod = mod_ref[0]
    xn = _rms_mod(x_ref[0], g_ref[...], mod[0:1], mod[1:2]).astype(BF16)
    cos = cos_ref[...]
    sin = sin_ref[...]

    z = _dot(xn, w_ref[...])

    def proj(lo, width):
        return z[:, lo:lo + width]

    o = 0
    rk_out[0] = (_rope(proj(o, RET_QK_W), cos, sin) * (RET_QK ** -0.5)).astype(BF16)
    o += RET_QK_W
    rv_out[0] = proj(o, RET_V_W).astype(BF16)
    o += RET_V_W
    sk_out[0] = _rope(proj(o, SWA_KV_W), cos, sin).astype(BF16)
    o += SWA_KV_W
    sv_out[0] = proj(o, SWA_KV_W).astype(BF16)
    o += SWA_KV_W
    rq_out[0] = _rope(proj(o, RET_QK_W), cos, sin).astype(BF16)
    o += RET_QK_W
    rg_out[0] = proj(o, RET_V_W)
    o += RET_V_W
    sq_out[0] = (_rope(proj(o, SWA_Q_W), cos, sin) * (SWA_HEAD_DIM ** -0.5 * LOG2E)).astype(BF16)


def _odd_in(xs, mod, g, w_in, cos, sin, tm):
    B, n, D = xs.shape
    tm = min(tm, n)
    widths = [(RET_QK_W, BF16), (RET_QK_W, BF16), (RET_V_W, BF16), (RET_V_W, F32),
              (SWA_Q_W, BF16), (SWA_KV_W, BF16), (SWA_KV_W, BF16)]
    return pl.pallas_call(
        _odd_in_kernel,
        grid=(B, n // tm),
        in_specs=[_tok_spec(tm, D), _mod_spec(mod), _resident(g.shape), _resident(w_in.shape),
                  _pos_spec(tm, cos), _pos_spec(tm, sin)],
        out_specs=[_tok_spec(tm, w) for w, _ in widths],
        out_shape=[jax.ShapeDtypeStruct((B, n, w), dt) for w, dt in widths],
        compiler_params=_params(2),
        name="odd_in",
    )(xs, mod, g, w_in, cos, sin)


def _ret_kernel(lg_ref, rq_ref, rk_ref, rv_ref, rg_ref, sf0_ref, sb0_ref, rn_ref,
                yr_ref, sfo_ref, sbo_ref, mask_scr, dq_scr, dk_scr, kv_scr, s_scr, *, nc):
    C = RET_CHUNK
    HQ = RET_QK
    row = lax.broadcasted_iota(jnp.int32, (C, C), 0)
    col = lax.broadcasted_iota(jnp.int32, (C, C), 1)
    diff = (row - col).astype(F32)
    pos = row.astype(F32)
    lo = col < HQ
    c_dec = []
    for h in range(RET_HEADS):
        lf = lg_ref[0, h]
        lb = lg_ref[1, h]
        mask_scr[h] = 0.5 * (jnp.where(diff >= 0, jnp.exp(lf * jnp.maximum(diff, 0.0)), 0.0)
                             + jnp.where(diff <= 0, jnp.exp(lb * jnp.maximum(-diff, 0.0)), 0.0))
        dq_scr[h] = jnp.where(lo, jnp.exp((pos + 1.0) * lf), jnp.exp((C - pos) * lb))
        dk_scr[h] = jnp.where(lo, jnp.exp((C - 1.0 - pos) * lf), jnp.exp(pos * lb))
        c_dec.append((jnp.exp(jnp.full((1, 1), C, F32) * lf), jnp.exp(jnp.full((1, 1), C, F32) * lb)))

    def doubled(ref, rows, p):
        ab = ref[0, rows, LANES * p:LANES * (p + 1)].astype(F32)
        ba = pltpu.roll(ab, HQ, 1)
        return jnp.where(lo, ab, ba), jnp.where(lo, ba, ab)

    def chunk_kv(c, carry):
        rows = pl.ds(pl.multiple_of(c * C, C), C)
        for p in range(RET_HEADS // 2):
            for e, k2 in enumerate(doubled(rk_ref, rows, p)):
                h = 2 * p + e
                kd = (k2 * dk_scr[h]).astype(BF16)
                kv_scr[c, h] = _dot_tn(kd, rv_ref[0, rows, RET_V * h:RET_V * (h + 1)])
        return carry

    lax.fori_loop(0, nc, chunk_kv, 0, unroll=min(4, nc))

    for h in range(RET_HEADS):
        c_dec_f, c_dec_b = c_dec[h]

        def fwd(c, s, h=h, c_dec_f=c_dec_f):
            s_scr[c, h, 0:HQ, :] = s.astype(BF16)
            return s * c_dec_f + kv_scr[c, h, 0:HQ, :]

        sfo_ref[0, h] = lax.fori_loop(0, nc, fwd, sf0_ref[0, h])

        def bwd(i, s, h=h, c_dec_b=c_dec_b):
            c = nc - 1 - i
            s_scr[c, h, HQ:2 * HQ, :] = s.astype(BF16)
            return s * c_dec_b + kv_scr[c, h, HQ:2 * HQ, :]

        sbo_ref[0, h] = lax.fori_loop(0, nc, bwd, sb0_ref[0, h])

    def chunk_out(c, carry):
        rows = pl.ds(pl.multiple_of(c * C, C), C)
        for p in range(RET_HEADS // 2):
            q2s = doubled(rq_ref, rows, p)
            k2s = doubled(rk_ref, rows, p)
            for e in range(2):
                h = 2 * p + e
                vc = slice(RET_V * h, RET_V * (h + 1))
                q2 = q2s[e]
                att = (_dot_nt(q2.astype(BF16), k2s[e].astype(BF16)) * mask_scr[h]).astype(BF16)
                y = _dot(att, rv_ref[0, rows, vc]) + _dot((q2 * dq_scr[h]).astype(BF16), s_scr[c, h])
                y = _rms(y, rn_ref[:, vc]) * jax.nn.silu(rg_ref[0, rows, vc])
                yr_ref[0, rows, vc] = y.astype(BF16)
        return carry

    lax.fori_loop(0, nc, chunk_out, 0, unroll=min(4, nc))


def _retention(lg, rq, rk, rv, rg, sf0, sb0, ret_norm):
    B, n, _ = rq.shape
    nc = n // RET_CHUNK
    st_shape = (B, RET_HEADS, RET_QK, RET_V)
    st_spec = pl.BlockSpec((1, RET_HEADS, RET_QK, RET_V), lambda b: (b, 0, 0, 0))

    def seq(width):
        return pl.BlockSpec((1, n, width), lambda b: (b, 0, 0))

    return pl.pallas_call(
        functools.partial(_ret_kernel, nc=nc),
        grid=(B,),
        in_specs=[_smem(), seq(RET_QK_W), seq(RET_QK_W), seq(RET_V_W), seq(RET_V_W), st_spec, st_spec,
                  _resident(ret_norm.shape)],
        out_specs=[seq(RET_V_W), st_spec, st_spec],
        out_shape=[jax.ShapeDtypeStruct((B, n, RET_V_W), BF16), jax.ShapeDtypeStruct(st_shape, F32),
                   jax.ShapeDtypeStruct(st_shape, F32)],
        scratch_shapes=[pltpu.VMEM((RET_HEADS, RET_CHUNK, RET_CHUNK), F32),
                        pltpu.VMEM((RET_HEADS, RET_CHUNK, LANES), F32),
                        pltpu.VMEM((RET_HEADS, RET_CHUNK, LANES), F32),
                        pltpu.VMEM((nc, RET_HEADS, 2 * RET_QK, RET_V), F32),
                        pltpu.VMEM((nc, RET_HEADS, 2 * RET_QK, RET_V), BF16)],
        compiler_params=_params(1),
        name="retention",
    )(lg, rq, rk, rv, rg, sf0, sb0, ret_norm)


def _swa_kernel(*refs, tq, n, local):
    if local:
        sink_ref, sq_ref, skx_ref, svx_ref, skh_ref, svh_ref, yr_ref, x_ref, mod_ref, wout_ref, o_ref = refs
    else:
        sink_ref, sq_ref, skh_ref, svh_ref, yr_ref, x_ref, mod_ref, wout_ref, o_ref = refs
    HD = SWA_HEAD_DIM
    groups = SWA_Q_HEADS // SWA_KV_HEADS
    gq = groups * tq
    q_t = sq_ref[0].astype(F32).T.astype(BF16)
    kh = skh_ref[0]
    vh_t = svh_ref[0].astype(F32).T.astype(BF16)
    if local:
        i = pl.program_id(1)
        win = tq + 2 * SWA_WINDOW
        start = pl.multiple_of(jnp.clip(i * tq - SWA_WINDOW, 0, n - win), SWA_WINDOW)
        kx = skx_ref[0, pl.ds(start, win), :]
        vx_t = svx_ref[0, pl.ds(start, win), :].astype(F32).T.astype(BF16)
        rel = (lax.broadcasted_iota(jnp.int32, (win, tq), 0) - lax.broadcasted_iota(jnp.int32, (win, tq), 1)
               + (start - i * tq))
        cap = jnp.where(jnp.abs(rel) <= SWA_WINDOW, jnp.inf, NEG_INF).astype(F32)
        cap = jnp.concatenate([cap] * groups, axis=1)
    sinks, s_hs, s_xs = [], [], []
    for j in range(SWA_KV_HEADS):
        kvc = slice(HD * j, HD * (j + 1))
        h0 = groups * j
        qj_t = jnp.concatenate([q_t[HD * (h0 + g):HD * (h0 + g + 1), :] for g in range(groups)], axis=1)
        sinks.append(jnp.concatenate([jnp.full((1, tq), sink_ref[h0 + g] * LOG2E, F32) for g in range(groups)], axis=1))
        s_hs.append(_dot(kh[:, kvc], qj_t))
        if local:
            s_xs.append(jnp.minimum(_dot(kx[:, kvc], qj_t), cap))
    ps = []
    for j in range(SWA_KV_HEADS):
        m = jnp.maximum(jnp.max(s_hs[j], axis=0, keepdims=True), sinks[j])
        if local:
            m = jnp.maximum(m, jnp.max(s_xs[j], axis=0, keepdims=True))
        p_h = jnp.exp2(s_hs[j] - m)
        denom = jnp.sum(p_h, axis=0, keepdims=True) + jnp.exp2(sinks[j] - m)
        p_x = None
        if local:
            p_x = jnp.exp2(s_xs[j] - m)
            denom = denom + jnp.sum(p_x, axis=0, keepdims=True)
            p_x = p_x.astype(BF16)
        ps.append((p_h.astype(BF16), p_x, denom))
    heads = []
    for j in range(SWA_KV_HEADS):
        kvc = slice(HD * j, HD * (j + 1))
        p_h, p_x, denom = ps[j]
        o_t = _dot(vh_t[kvc, :], p_h)
        if local:
            o_t = o_t + _dot(vx_t[kvc, :], p_x)
        o_t = o_t / denom
        heads += [o_t[:, g * tq:(g + 1) * tq] for g in range(groups)]
    o = jnp.concatenate(heads, axis=0).T.astype(BF16)
    y = _dot(yr_ref[0], wout_ref[0:RET_V_W, :]) + _dot(o, wout_ref[RET_V_W:RET_V_W + SWA_Q_W, :])
    o_ref[0] = x_ref[0] + mod_ref[0][2:3] * y


def _swa(sink, sq, local_kv, ctx_kv, yr, xs, mod, w_out, tq):
    B, n, D = xs.shape
    tq = min(tq, n)
    local = local_kv is not None
    if local:
        assert n >= tq + 2 * SWA_WINDOW
    nh = ctx_kv[0].shape[1]
    kv_specs = ([_seq_spec(n, SWA_KV_W)] * 2 if local else []) + [_seq_spec(nh, SWA_KV_W)] * 2
    kv_args = (list(local_kv) if local else []) + list(ctx_kv)
    return pl.pallas_call(
        functools.partial(_swa_kernel, tq=tq, n=n, local=local),
        grid=(B, n // tq),
        in_specs=[_smem(), _tok_spec(tq, SWA_Q_W)] + kv_specs
                 + [_tok_spec(tq, RET_V_W), _tok_spec(tq, D), _mod_spec(mod), _resident(w_out.shape)],
        out_specs=_tok_spec(tq, D),
        out_shape=jax.ShapeDtypeStruct((B, n, D), F32),
        compiler_params=_params(2),
        name="swa_local" if local else "swa_ctx",
    )(sink, sq, *kv_args, yr, xs, mod, w_out)


def _rope_tables(n):
    t = np.arange(n)
    row = (t // GRID_W).astype(np.float32)
    col = (t % GRID_W).astype(np.float32)
    n_freq = 16
    freqs = jnp.asarray(ROPE_THETA, F32) ** (-jnp.arange(n_freq, dtype=F32) / n_freq)
    ang = jnp.concatenate([jnp.asarray(row)[:, None] * freqs, jnp.asarray(col)[:, None] * freqs], axis=-1)
    cos, sin = jnp.cos(ang), jnp.sin(ang)
    cos2 = jnp.tile(jnp.concatenate([cos, cos], axis=-1), (1, 2))
    sin2 = jnp.tile(jnp.concatenate([-sin, sin], axis=-1), (1, 2))
    return cos2, sin2


def _even_weights(ab_in, ab_out, q_norm, kv_norm, wq_b, wkv_b, v_norm, w_s, b_s):
    kv_w = MLA_LORA + MLA_ROPE
    w_in = jnp.concatenate([ab_in[:, :MLA_LORA], ab_in[:, kv_w:kv_w + MLA_LORA], ab_in[:, kv_w + MLA_LORA:],
                            ab_in[:, MLA_LORA:kv_w], jnp.zeros((ab_in.shape[0], LANES - MLA_ROPE), ab_in.dtype)],
                           axis=1).astype(BF16)
    wq = wq_b.reshape(MLA_LORA, MLA_HEADS, MLA_NOPE + MLA_ROPE)
    wqn = wq[:, :, :MLA_NOPE].reshape(MLA_LORA, MLA_HEADS * MLA_NOPE).astype(BF16)
    wqr = jnp.pad(wq[:, :, MLA_NOPE:], ((0, 0), (0, 0), (0, LANES - MLA_ROPE)))
    wqr = wqr.reshape(MLA_LORA, MLA_HEADS * LANES).astype(BF16)
    wkv = wkv_b.reshape(MLA_LORA, MLA_HEADS, MLA_NOPE + MLA_V)
    wkn = wkv[:, :, :MLA_NOPE].reshape(MLA_LORA, MLA_HEADS * MLA_NOPE).astype(BF16)
    wv = wkv[:, :, MLA_NOPE:].reshape(MLA_LORA, MLA_HEADS * MLA_V).astype(BF16)
    bs = jnp.broadcast_to(b_s[:, :, None], (CMLP_GROUPS, CMLP_CHUNK, CMLP_DIM)).astype(F32)
    return dict(w_in=w_in, wqn=wqn, wqr=wqr, wkn=wkn, wv=wv, q_norm=q_norm[None, :], kv_norm=kv_norm[None, :],
                v_norm=v_norm[None, :], ws=w_s.astype(BF16), bs=bs, w_out=ab_out.astype(BF16))


def kernel(x, c, ctx, c_ctx, ada_w, ada_b, norm_mix, norm_ffn, norm_final, ffn_in, ffn_out, ab_in, ab_out,
           mla_q_norm, mla_kv_norm, mla_wq_b, mla_wkv_b, cmlp_v_norm, cmlp_ws, cmlp_bs, cd_in, cd_out,
           ret_decay_fwd, ret_decay_bwd, ret_norm, swa_sink):
    B, n, D = x.shape
    nh = ctx.shape[1]
    depth = ada_w.shape[0]
    tm = 512
    tq_mla = 512
    tq_swa = 256

    rows = -(-(B + 1) // 8) * 8
    cond_rows = jnp.concatenate([c, c_ctx[None, :], jnp.zeros((rows - B - 1, D), F32)], axis=0)
    mods = _ada(cond_rows, ada_w, ada_b).reshape(depth, rows, 6, D)
    mod_x = mods[:, :B]
    mod_h = mods[:, B:B + 1]

    cos_x, sin_x = _rope_tables(n)
    cos_h, sin_h = jnp.ones((tm, LANES), F32), jnp.zeros((tm, LANES), F32)

    def flat(a):
        return a.reshape(1, B * nh, a.shape[-1])

    def unflat(a):
        return a.reshape(B, nh, a.shape[-1])

    pe_mask = (jnp.arange(LANES) < MLA_ROPE).astype(F32)[None, :]
    cos_xm, sin_xm = cos_x * pe_mask, sin_x * pe_mask

    h = ctx
    for layer in range(depth):
        last = layer == depth - 1
        j = layer // 2
        g_mix = norm_mix[layer][None, :]
        mx, mh = mod_x[layer], mod_h[layer]
        if layer % 2 == 0:
            w = _even_weights(ab_in[j], ab_out[j], mla_q_norm[j], mla_kv_norm[j], mla_wq_b[j], mla_wkv_b[j],
                              cmlp_v_norm[j], cmlp_ws[j], cmlp_bs[j])
            qx, kx, vx, cmx = _even_in(x, mx, g_mix, w, cos_xm, sin_xm, tm)
            qh, kh, vh, cmh = [unflat(a) for a in _even_in(flat(h), mh, g_mix, w, cos_h, sin_h, tm)]
            x = _mla_attn(qx, [(kx, vx), (kh, vh)], cmx, x, mx, w["w_out"], tq_mla)
            if not last:
                h = _mla_attn(qh, [(kh, vh)], cmh, h, mh, w["w_out"], tq_mla)
        else:
            w_in = cd_in[j].astype(BF16)
            w_out = cd_out[j].astype(BF16)
            lg = jnp.stack([jax.nn.log_sigmoid(ret_decay_fwd[j].astype(F32)),
                            jax.nn.log_sigmoid(ret_decay_bwd[j].astype(F32))])
            rn = ret_norm[j][None, :]
            rqx, rkx, rvx, rgx, sqx, skx, svx = _odd_in(x, mx, g_mix, w_in, cos_x, sin_x, tm)
            rqh, rkh, rvh, rgh, sqh, skh, svh = [unflat(a) for a in
                                                 _odd_in(flat(h), mh, g_mix, w_in, cos_h, sin_h, tm)]
            s0 = jnp.zeros((B, RET_HEADS, RET_QK, RET_V), F32)
            yrh, s_f, s_b = _retention(lg, rqh, rkh, rvh, rgh, s0, s0, rn)
            yrx, _, _ = _retention(lg, rqx, rkx, rvx, rgx, s_f, s_b, rn)
            sink = swa_sink[j].astype(F32)
            x = _swa(sink, sqx, (skx, svx), (skh, svh), yrx, x, mx, w_out, tq_swa)
            if not last:
                h = _swa(sink, sqh, None, (skh, svh), yrh, h, mh, w_out, tq_swa)
        g_ffn = norm_ffn[layer][None, :]
        w1 = ffn_in[layer].astype(BF16)
        w2 = ffn_out[layer].astype(BF16)
        x = _ffn(x, mx, g_ffn, w1, w2, tm, g_final=norm_final[None, :] if last else None)
        if not last:
            h = unflat(_ffn(flat(h), mh, g_ffn, w1, w2, tm))
    return x
```

```python
import functools

import numpy as np
import jax
import jax.numpy as jnp
from jax import lax
from jax.experimental import pallas as pl
from jax.experimental.pallas import tpu as pltpu

F32 = jnp.float32
BF16 = jnp.bfloat16

GRID_W = 64
ROPE_THETA = 10000.0
EPS = 1e-6
NEG_INF = -1e30
LOG2E = 1.4426950408889634

LANES = 128

MLA_HEADS = 4
MLA_LORA = 256
MLA_NOPE = 128
MLA_ROPE = 64
MLA_V = 128
MLA_QK_PAD = 256
CMLP_GROUPS = 4
CMLP_DIM = 128
CMLP_CHUNK = 128
CMLP_WIDTH = CMLP_GROUPS * CMLP_DIM
RET_HEADS = 4
RET_QK = 64
RET_V = 128
RET_CHUNK = 128
RET_QK_W = RET_HEADS * RET_QK
RET_V_W = RET_HEADS * RET_V
SWA_Q_HEADS = 8
SWA_KV_HEADS = 2
SWA_HEAD_DIM = 64
SWA_WINDOW = 128
SWA_Q_W = SWA_Q_HEADS * SWA_HEAD_DIM
SWA_KV_W = SWA_KV_HEADS * SWA_HEAD_DIM

VMEM_LIMIT_BYTES = 56 * 2**20

ODD_SUB = 256
FFN_SUB = 128
SWA_SUB = 128

_NT = (((1,), (1,)), ((), ()))
_TN = (((0,), (0,)), ((), ()))


def _dot(a, b):
    return jnp.dot(a, b, preferred_element_type=F32)


def _dot_nt(a, b):
    return lax.dot_general(a, b, _NT, preferred_element_type=F32)


def _dot_tn(a, b):
    return lax.dot_general(a, b, _TN, preferred_element_type=F32)


def _rms(x, g):
    return x * lax.rsqrt(jnp.mean(x * x, axis=-1, keepdims=True) + EPS) * g


def _rms_mod(x, g, shift, scale):
    return x * lax.rsqrt(jnp.mean(x * x, axis=-1, keepdims=True) + EPS) * (g * (1.0 + scale)) + shift


def _rope128(x, cos, sin):
    lane = lax.broadcasted_iota(jnp.int32, x.shape, 1)
    first = (lane & 32) == 0
    swapped = jnp.where(first, pltpu.roll(x, LANES - 32, 1), pltpu.roll(x, 32, 1))
    return x * cos + swapped * sin


def _rope(x, cos, sin):
    cols = [_rope128(x[:, i:i + LANES], cos, sin) for i in range(0, x.shape[1], LANES)]
    return cols[0] if len(cols) == 1 else jnp.concatenate(cols, axis=-1)


def _params(n_grid):
    return pltpu.CompilerParams(dimension_semantics=("parallel",) * n_grid,
                                vmem_limit_bytes=VMEM_LIMIT_BYTES)


def _resident(shape):
    nd = len(shape)
    return pl.BlockSpec(shape, lambda *_: (0,) * nd, pipeline_mode=pl.Buffered(1))


def _smem():
    return pl.BlockSpec(memory_space=pltpu.SMEM)


def _tok_spec(tm, width):
    return pl.BlockSpec((1, tm, width), lambda b, i: (b, i, 0))


def _seq_spec(n, width):
    return pl.BlockSpec((1, n, width), lambda b, i: (b, 0, 0))


def _mod_spec(mod):
    if mod.shape[0] == 1:
        return pl.BlockSpec((1,) + mod.shape[1:], lambda b, i: (0, 0, 0))
    return pl.BlockSpec((1,) + mod.shape[1:], lambda b, i: (b, 0, 0))


def _pos_spec(tm, table):
    nblk = table.shape[0] // tm
    return pl.BlockSpec((tm, LANES), lambda b, i: (i % nblk, 0))


def _ada_kernel(c_ref, w_ref, b_ref, o_ref):
    cond = jax.nn.silu(c_ref[...]).astype(BF16)
    o_ref[0] = _dot(cond, w_ref[0].astype(BF16)) + b_ref[0]


def _ada(cond_rows, ada_w, ada_b):
    L, D, N = ada_w.shape
    R = cond_rows.shape[0]
    tn = 1536
    return pl.pallas_call(
        _ada_kernel,
        grid=(L, N // tn),
        in_specs=[pl.BlockSpec((R, D), lambda l, j: (0, 0)),
                  pl.BlockSpec((1, D, tn), lambda l, j: (l, 0, j)),
                  pl.BlockSpec((1, 1, tn), lambda l, j: (l, 0, j))],
        out_specs=pl.BlockSpec((1, R, tn), lambda l, j: (l, 0, j)),
        out_shape=jax.ShapeDtypeStruct((L, R, N), F32),
        compiler_params=_params(2),
        name="ada_mod",
    )(cond_rows, ada_w, ada_b.reshape(L, 1, N))


def _even_in_kernel(x_ref, mod_ref, g_ref, win_ref, wqn_ref, wqr_ref, wkn_ref, wv_ref,
                    qn_ref, kvn_ref, vn_ref, ws_ref, bs_ref, cos_ref, sin_ref,
                    q_out, k_out, v_out, cm_out, *, tm, scale):
    mod = mod_ref[0]
    xn = _rms_mod(x_ref[0], g_ref[...], mod[0:1], mod[1:2]).astype(BF16)
    cos = cos_ref[...]
    sin = sin_ref[...]
    L = MLA_LORA
    kv_lat = _dot(xn, win_ref[:, 0:L])
    q_lat = _dot(xn, win_ref[:, L:2 * L])
    uv_off = 2 * L
    pe_off = uv_off + 2 * CMLP_WIDTH
    kpe = _rope128(_dot(xn, win_ref[:, pe_off:pe_off + LANES]), cos, sin).astype(BF16)
    kvn = _rms(kv_lat, kvn_ref[...]).astype(BF16)
    qn = _rms(q_lat, qn_ref[...]).astype(BF16)
    k_nope = _dot(kvn, wkn_ref[...])
    v_out[0] = _dot(kvn, wv_ref[...]).astype(BF16)
    q_nope = _dot(qn, wqn_ref[...]) * scale
    q_rope = _dot(qn, wqr_ref[...]) * scale
    for h in range(MLA_HEADS):
        src = slice(LANES * h, LANES * (h + 1))
        lo = MLA_QK_PAD * h
        q_out[0, :, lo:lo + LANES] = q_nope[:, src].astype(BF16)
        q_out[0, :, lo + LANES:lo + 2 * LANES] = _rope128(q_rope[:, src], cos, sin).astype(BF16)
        k_out[0, :, lo:lo + LANES] = k_nope[:, src].astype(BF16)
        k_out[0, :, lo + LANES:lo + 2 * LANES] = kpe
    uv = jax.nn.gelu(_dot(xn, win_ref[:, uv_off:uv_off + 2 * CMLP_WIDTH]))
    for g in range(CMLP_GROUPS):
        col = slice(CMLP_DIM * g, CMLP_DIM * (g + 1))
        vg = uv[:, CMLP_WIDTH + CMLP_DIM * g:CMLP_WIDTH + CMLP_DIM * (g + 1)]
        vg = _rms(vg, vn_ref[:, col]).astype(BF16)
        ug = uv[:, col]
        wsg = ws_ref[g]
        bsg = bs_ref[g]
        for c in range(tm // CMLP_CHUNK):
            rows = slice(CMLP_CHUNK * c, CMLP_CHUNK * (c + 1))
            mixed = _dot(wsg, vg[rows]) + bsg
            cm_out[0, rows, col] = (ug[rows] * mixed).astype(BF16)


def _even_in(xs, mod, g, w, cos, sin, tm):
    B, n, D = xs.shape
    tm = min(tm, n)
    qk_w = MLA_HEADS * MLA_QK_PAD
    v_w = MLA_HEADS * MLA_V
    scale = float((MLA_NOPE + MLA_ROPE) ** -0.5 * LOG2E)
    weights = (w["w_in"], w["wqn"], w["wqr"], w["wkn"], w["wv"], w["q_norm"], w["kv_norm"],
               w["v_norm"], w["ws"], w["bs"])
    return pl.pallas_call(
        functools.partial(_even_in_kernel, tm=tm, scale=scale),
        grid=(B, n // tm),
        in_specs=[_tok_spec(tm, D), _mod_spec(mod), _resident(g.shape)]
                 + [_resident(a.shape) for a in weights] + [_pos_spec(tm, cos), _pos_spec(tm, sin)],
        out_specs=[_tok_spec(tm, qk_w), _tok_spec(tm, qk_w), _tok_spec(tm, v_w), _tok_spec(tm, CMLP_WIDTH)],
        out_shape=[jax.ShapeDtypeStruct((B, n, qk_w), BF16), jax.ShapeDtypeStruct((B, n, qk_w), BF16),
                   jax.ShapeDtypeStruct((B, n, v_w), BF16), jax.ShapeDtypeStruct((B, n, CMLP_WIDTH), BF16)],
        compiler_params=_params(2),
        name="even_in",
    )(xs, mod, g, *weights, cos, sin)


def _mla_attn_kernel(*refs, n_pieces):
    q_ref = refs[0]
    kv_refs = refs[1:1 + 2 * n_pieces]
    cm_ref, x_ref, mod_ref, wout_ref, o_ref = refs[1 + 2 * n_pieces:]
    q = q_ref[0]
    ow = MLA_HEADS * MLA_V
    y_cm = _dot(cm_ref[0], wout_ref[ow:ow + CMLP_WIDTH, :])

    def score(h):
        qk = slice(MLA_QK_PAD * h, MLA_QK_PAD * (h + 1))
        return [_dot_nt(q[:, qk], kv_refs[2 * p][0, :, qk]) for p in range(n_pieces)]

    heads = []
    nxt = score(0)
    for h in range(MLA_HEADS):
        scores = nxt
        if h + 1 < MLA_HEADS:
            nxt = score(h + 1)
        vc = slice(MLA_V * h, MLA_V * (h + 1))
        m = functools.reduce(jnp.maximum, [jnp.max(s, axis=-1, keepdims=True) for s in scores])
        probs = [jnp.exp2(s - m) for s in scores]
        denom = sum(jnp.sum(p, axis=-1, keepdims=True) for p in probs)
        o = sum(_dot(probs[p].astype(BF16), kv_refs[2 * p + 1][0, :, vc]) for p in range(n_pieces))
        heads.append(o / denom)
    o = jnp.concatenate(heads, axis=-1).astype(BF16)
    y = _dot(o, wout_ref[0:ow, :]) + y_cm
    o_ref[0] = x_ref[0] + mod_ref[0][2:3] * y


def _mla_attn(q, pieces, cm, xs, mod, w_out, tq):
    B, n, D = xs.shape
    tq = min(tq, n)
    kv_specs, kv_args = [], []
    for k, v in pieces:
        kv_specs += [_seq_spec(k.shape[1], k.shape[2]), _seq_spec(v.shape[1], v.shape[2])]
        kv_args += [k, v]
    return pl.pallas_call(
        functools.partial(_mla_attn_kernel, n_pieces=len(pieces)),
        grid=(B, n // tq),
        in_specs=[_tok_spec(tq, q.shape[2])] + kv_specs
                 + [_tok_spec(tq, cm.shape[2]), _tok_spec(tq, D), _mod_spec(mod), _resident(w_out.shape)],
        out_specs=_tok_spec(tq, D),
        out_shape=jax.ShapeDtypeStruct((B, n, D), F32),
        compiler_params=_params(2),
        name="mla_attn",
    )(q, *kv_args, cm, xs, mod, w_out)


def _ffn_kernel(*refs, hidden, tm, sub, final):
    if final:
        x_ref, mod_ref, g_ref, win_ref, wout_ref, gf_ref, o_ref = refs
    else:
        x_ref, mod_ref, g_ref, win_ref, wout_ref, o_ref = refs
    mod = mod_ref[0]
    for r0 in range(0, tm, sub):
        rows = slice(r0, r0 + sub)
        x = x_ref[0, rows, :]
        xn = _rms_mod(x, g_ref[...], mod[3:4], mod[4:5]).astype(BF16)
        a = _dot(xn, win_ref[:, 0:hidden])
        b = _dot(xn, win_ref[:, hidden:2 * hidden])
        acc = _dot((jax.nn.silu(a) * b).astype(BF16), wout_ref[...])
        out = x + mod[5:6] * acc
        if final:
            out = _rms(out, gf_ref[...])
        o_ref[0, rows, :] = out


def _ffn(xs, mod, g, w_in, w_out, tm, g_final=None):
    B, n, D = xs.shape
    tm = min(tm, n)
    hidden = w_out.shape[0]
    final = g_final is not None
    extra = [g_final] if final else []
    return pl.pallas_call(
        functools.partial(_ffn_kernel, hidden=hidden, tm=tm, sub=min(FFN_SUB, tm), final=final),
        grid=(B, n // tm),
        in_specs=[_tok_spec(tm, D), _mod_spec(mod), _resident(g.shape), _resident(w_in.shape),
                  _resident(w_out.shape)] + [_resident(a.shape) for a in extra],
        out_specs=_tok_spec(tm, D),
        out_shape=jax.ShapeDtypeStruct((B, n, D), F32),
        compiler_params=_params(2),
        name="ffn",
    )(xs, mod, g, w_in, w_out, *extra)


def _odd_in_kernel(x_ref, mod_ref, g_ref, w_ref, cos_ref, sin_ref,
                   rq_out, rk_out, rv_out, rg_out, sq_out, sk_out, sv_out, *, tm, sub):
    mod = mod_ref[0]
    for r0 in range(0, tm, sub):
        rows = slice(r0, r0 + sub)
        xn = _rms_mod(x_ref[0, rows, :], g_ref[...], mod[0:1], mod[1:2]).astype(BF16)
        cos = cos_ref[rows, :]
        sin = sin_ref[rows, :]
        z = _dot(xn, w_ref[...])

        def proj(lo, width):
            return z[:, lo:lo + width]

        o = 0
        rk_out[0, rows, :] = (_rope(proj(o, RET_QK_W), cos, sin) * (RET_QK ** -0.5)).astype(BF16)
        o += RET_QK_W
        rv_out[0, rows, :] = proj(o, RET_V_W).astype(BF16)
        o += RET_V_W
        sk_out[0, rows, :] = _rope(proj(o, SWA_KV_W), cos, sin).astype(BF16)
        o += SWA_KV_W
        sv_out[0, rows, :] = proj(o, SWA_KV_W).astype(BF16)
        o += SWA_KV_W
        rq_out[0, rows, :] = _rope(proj(o, RET_QK_W), cos, sin).astype(BF16)
        o += RET_QK_W
        rg_out[0, rows, :] = proj(o, RET_V_W)
        o += RET_V_W
        sq_out[0, rows, :] = (_rope(proj(o, SWA_Q_W), cos, sin) * (SWA_HEAD_DIM ** -0.5 * LOG2E)).astype(BF16)


def _odd_in(xs, mod, g, w_in, cos, sin, tm):
    B, n, D = xs.shape
    tm = min(tm, n)
    widths = [(RET_QK_W, BF16), (RET_QK_W, BF16), (RET_V_W, BF16), (RET_V_W, F32),
              (SWA_Q_W, BF16), (SWA_KV_W, BF16), (SWA_KV_W, BF16)]
    return pl.pallas_call(
        functools.partial(_odd_in_kernel, tm=tm, sub=min(ODD_SUB, tm)),
        grid=(B, n // tm),
        in_specs=[_tok_spec(tm, D), _mod_spec(mod), _resident(g.shape), _resident(w_in.shape),
                  _pos_spec(tm, cos), _pos_spec(tm, sin)],
        out_specs=[_tok_spec(tm, w) for w, _ in widths],
        out_shape=[jax.ShapeDtypeStruct((B, n, w), dt) for w, dt in widths],
        compiler_params=_params(2),
        name="odd_in",
    )(xs, mod, g, w_in, cos, sin)


def _ret_kernel(lg_ref, rq_ref, rk_ref, rv_ref, rg_ref, sf0_ref, sb0_ref, rn_ref,
                yr_ref, sfo_ref, sbo_ref, mask_scr, dq_scr, dk_scr, kv_scr, s_scr, *, nc):
    C = RET_CHUNK
    HQ = RET_QK
    row = lax.broadcasted_iota(jnp.int32, (C, C), 0)
    col = lax.broadcasted_iota(jnp.int32, (C, C), 1)
    diff = (row - col).astype(F32)
    pos = row.astype(F32)
    lo = col < HQ
    c_dec = []
    for h in range(RET_HEADS):
        lf = lg_ref[0, h]
        lb = lg_ref[1, h]
        mask_scr[h] = 0.5 * (jnp.where(diff >= 0, jnp.exp(lf * jnp.maximum(diff, 0.0)), 0.0)
                             + jnp.where(diff <= 0, jnp.exp(lb * jnp.maximum(-diff, 0.0)), 0.0))
        dq_scr[h] = jnp.where(lo, jnp.exp((pos + 1.0) * lf), jnp.exp((C - pos) * lb))
        dk_scr[h] = jnp.where(lo, jnp.exp((C - 1.0 - pos) * lf), jnp.exp(pos * lb))
        c_dec.append((jnp.exp(jnp.full((1, 1), C, F32) * lf), jnp.exp(jnp.full((1, 1), C, F32) * lb)))

    def doubled(ref, rows, p):
        ab = ref[0, rows, LANES * p:LANES * (p + 1)].astype(F32)
        ba = pltpu.roll(ab, HQ, 1)
        return jnp.where(lo, ab, ba), jnp.where(lo, ba, ab)

    def chunk_kv(c, carry):
        rows = pl.ds(pl.multiple_of(c * C, C), C)
        for p in range(RET_HEADS // 2):
            for e, k2 in enumerate(doubled(rk_ref, rows, p)):
                h = 2 * p + e
                kd = (k2 * dk_scr[h]).astype(BF16)
                kv_scr[c, h] = _dot_tn(kd, rv_ref[0, rows, RET_V * h:RET_V * (h + 1)])
        return carry

    lax.fori_loop(0, nc, chunk_kv, 0, unroll=min(4, nc))

    for h in range(RET_HEADS):
        c_dec_f, c_dec_b = c_dec[h]

        def fwd(c, s, h=h, c_dec_f=c_dec_f):
            s_scr[c, h, 0:HQ, :] = s.astype(BF16)
            return s * c_dec_f + kv_scr[c, h, 0:HQ, :]

        sfo_ref[0, h] = lax.fori_loop(0, nc, fwd, sf0_ref[0, h])

        def bwd(i, s, h=h, c_dec_b=c_dec_b):
            c = nc - 1 - i
            s_scr[c, h, HQ:2 * HQ, :] = s.astype(BF16)
            return s * c_dec_b + kv_scr[c, h, HQ:2 * HQ, :]

        sbo_ref[0, h] = lax.fori_loop(0, nc, bwd, sb0_ref[0, h])

    def chunk_out(c, carry):
        rows = pl.ds(pl.multiple_of(c * C, C), C)
        for p in range(RET_HEADS // 2):
            q2s = doubled(rq_ref, rows, p)
            k2s = doubled(rk_ref, rows, p)
            for e in range(2):
                h = 2 * p + e
                vc = slice(RET_V * h, RET_V * (h + 1))
                q2 = q2s[e]
                att = (_dot_nt(q2.astype(BF16), k2s[e].astype(BF16)) * mask_scr[h]).astype(BF16)
                y = _dot(att, rv_ref[0, rows, vc]) + _dot((q2 * dq_scr[h]).astype(BF16), s_scr[c, h])
                y = _rms(y, rn_ref[:, vc]) * jax.nn.silu(rg_ref[0, rows, vc])
                yr_ref[0, rows, vc] = y.astype(BF16)
        return carry

    lax.fori_loop(0, nc, chunk_out, 0, unroll=min(4, nc))


def _retention(lg, rq, rk, rv, rg, sf0, sb0, ret_norm):
    B, n, _ = rq.shape
    nc = n // RET_CHUNK
    st_shape = (B, RET_HEADS, RET_QK, RET_V)
    st_spec = pl.BlockSpec((1, RET_HEADS, RET_QK, RET_V), lambda b: (b, 0, 0, 0))

    def seq(width):
        return pl.BlockSpec((1, n, width), lambda b: (b, 0, 0))

    return pl.pallas_call(
        functools.partial(_ret_kernel, nc=nc),
        grid=(B,),
        in_specs=[_smem(), seq(RET_QK_W), seq(RET_QK_W), seq(RET_V_W), seq(RET_V_W), st_spec, st_spec,
                  _resident(ret_norm.shape)],
        out_specs=[seq(RET_V_W), st_spec, st_spec],
        out_shape=[jax.ShapeDtypeStruct((B, n, RET_V_W), BF16), jax.ShapeDtypeStruct(st_shape, F32),
                   jax.ShapeDtypeStruct(st_shape, F32)],
        scratch_shapes=[pltpu.VMEM((RET_HEADS, RET_CHUNK, RET_CHUNK), F32),
                        pltpu.VMEM((RET_HEADS, RET_CHUNK, LANES), F32),
                        pltpu.VMEM((RET_HEADS, RET_CHUNK, LANES), F32),
                        pltpu.VMEM((nc, RET_HEADS, 2 * RET_QK, RET_V), F32),
                        pltpu.VMEM((nc, RET_HEADS, 2 * RET_QK, RET_V), BF16)],
        compiler_params=_params(1),
        name="retention",
    )(lg, rq, rk, rv, rg, sf0, sb0, ret_norm)


def _swa_kernel(*refs, tq, sub, n, local):
    if local:
        sink_ref, sq_ref, skx_ref, svx_ref, skh_ref, svh_ref, yr_ref, x_ref, mod_ref, wout_ref, o_ref = refs
    else:
        sink_ref, sq_ref, skh_ref, svh_ref, yr_ref, x_ref, mod_ref, wout_ref, o_ref = refs
    HD = SWA_HEAD_DIM
    groups = SWA_Q_HEADS // SWA_KV_HEADS
    kh = skh_ref[0]
    vh_t = svh_ref[0].astype(F32).T.astype(BF16)
    q_t = sq_ref[0].astype(F32).T.astype(BF16)
    y_ret = _dot(yr_ref[0], wout_ref[0:RET_V_W, :])
    blocks = []
    for r0 in range(0, tq, sub):
        if local:
            base = pl.program_id(1) * tq + r0
            win = sub + 2 * SWA_WINDOW
            start = pl.multiple_of(jnp.clip(base - SWA_WINDOW, 0, n - win), SWA_WINDOW)
            kx = skx_ref[0, pl.ds(start, win), :]
            vx_t = svx_ref[0, pl.ds(start, win), :].astype(F32).T.astype(BF16)
            rel = (lax.broadcasted_iota(jnp.int32, (win, sub), 0) - lax.broadcasted_iota(jnp.int32, (win, sub), 1)
                   + (start - base))
            cap = jnp.where(jnp.abs(rel) <= SWA_WINDOW, jnp.inf, NEG_INF).astype(F32)
            cap = jnp.concatenate([cap] * groups, axis=1)
        sinks, s_hs, s_xs = [], [], []
        for j in range(SWA_KV_HEADS):
            kvc = slice(HD * j, HD * (j + 1))
            h0 = groups * j
            qj_t = jnp.concatenate([q_t[HD * (h0 + g):HD * (h0 + g + 1), r0:r0 + sub] for g in range(groups)], axis=1)
            sinks.append(jnp.concatenate([jnp.full((1, sub), sink_ref[h0 + g] * LOG2E, F32) for g in range(groups)], axis=1))
            s_hs.append(_dot(kh[:, kvc], qj_t))
            if local:
                s_xs.append(jnp.minimum(_dot(kx[:, kvc], qj_t), cap))
        ps = []
        for j in range(SWA_KV_HEADS):
            m = jnp.maximum(jnp.max(s_hs[j], axis=0, keepdims=True), sinks[j])
            if local:
                m = jnp.maximum(m, jnp.max(s_xs[j], axis=0, keepdims=True))
            p_h = jnp.exp2(s_hs[j] - m)
            denom = jnp.sum(p_h, axis=0, keepdims=True) + jnp.exp2(sinks[j] - m)
            p_x = None
            if local:
                p_x = jnp.exp2(s_xs[j] - m)
                denom = denom + jnp.sum(p_x, axis=0, keepdims=True)
                p_x = p_x.astype(BF16)
            ps.append((p_h.astype(BF16), p_x, denom))
        heads = []
        for j in range(SWA_KV_HEADS):
            kvc = slice(HD * j, HD * (j + 1))
            p_h, p_x, denom = ps[j]
            o_t = _dot(vh_t[kvc, :], p_h)
            if local:
                o_t = o_t + _dot(vx_t[kvc, :], p_x)
            o_t = o_t / denom
            heads += [o_t[:, g * sub:(g + 1) * sub] for g in range(groups)]
        blocks.append(jnp.concatenate(heads, axis=0).T.astype(BF16))
    o = blocks[0] if len(blocks) == 1 else jnp.concatenate(blocks, axis=0)
    y = y_ret + _dot(o, wout_ref[RET_V_W:RET_V_W + SWA_Q_W, :])
    o_ref[0] = x_ref[0] + mod_ref[0][2:3] * y


def _swa(sink, sq, local_kv, ctx_kv, yr, xs, mod, w_out, tq):
    B, n, D = xs.shape
    tq = min(tq, n)
    local = local_kv is not None
    if local:
        assert n >= min(SWA_SUB, tq) + 2 * SWA_WINDOW
    nh = ctx_kv[0].shape[1]
    kv_specs = ([_seq_spec(n, SWA_KV_W)] * 2 if local else []) + [_seq_spec(nh, SWA_KV_W)] * 2
    kv_args = (list(local_kv) if local else []) + list(ctx_kv)
    return pl.pallas_call(
        functools.partial(_swa_kernel, tq=tq, sub=min(SWA_SUB, tq) if local else tq, n=n, local=local),
        grid=(B, n // tq),
        in_specs=[_smem(), _tok_spec(tq, SWA_Q_W)] + kv_specs
                 + [_tok_spec(tq, RET_V_W), _tok_spec(tq, D), _mod_spec(mod), _resident(w_out.shape)],
        out_specs=_tok_spec(tq, D),
        out_shape=jax.ShapeDtypeStruct((B, n, D), F32),
        compiler_params=_params(2),
        name="swa_local" if local else "swa_ctx",
    )(sink, sq, *kv_args, yr, xs, mod, w_out)


def _rope_tables(n):
    t = np.arange(n)
    row = (t // GRID_W).astype(np.float32)
    col = (t % GRID_W).astype(np.float32)
    n_freq = 16
    freqs = jnp.asarray(ROPE_THETA, F32) ** (-jnp.arange(n_freq, dtype=F32) / n_freq)
    ang = jnp.concatenate([jnp.asarray(row)[:, None] * freqs, jnp.asarray(col)[:, None] * freqs], axis=-1)
    cos, sin = jnp.cos(ang), jnp.sin(ang)
    cos2 = jnp.tile(jnp.concatenate([cos, cos], axis=-1), (1, 2))
    sin2 = jnp.tile(jnp.concatenate([-sin, sin], axis=-1), (1, 2))
    return cos2, sin2


def _even_weights(ab_in, ab_out, q_norm, kv_norm, wq_b, wkv_b, v_norm, w_s, b_s):
    kv_w = MLA_LORA + MLA_ROPE
    w_in = jnp.concatenate([ab_in[:, :MLA_LORA], ab_in[:, kv_w:kv_w + MLA_LORA], ab_in[:, kv_w + MLA_LORA:],
                            ab_in[:, MLA_LORA:kv_w], jnp.zeros((ab_in.shape[0], LANES - MLA_ROPE), ab_in.dtype)],
                           axis=1).astype(BF16)
    wq = wq_b.reshape(MLA_LORA, MLA_HEADS, MLA_NOPE + MLA_ROPE)
    wqn = wq[:, :, :MLA_NOPE].reshape(MLA_LORA, MLA_HEADS * MLA_NOPE).astype(BF16)
    wqr = jnp.pad(wq[:, :, MLA_NOPE:], ((0, 0), (0, 0), (0, LANES - MLA_ROPE)))
    wqr = wqr.reshape(MLA_LORA, MLA_HEADS * LANES).astype(BF16)
    wkv = wkv_b.reshape(MLA_LORA, MLA_HEADS, MLA_NOPE + MLA_V)
    wkn = wkv[:, :, :MLA_NOPE].reshape(MLA_LORA, MLA_HEADS * MLA_NOPE).astype(BF16)
    wv = wkv[:, :, MLA_NOPE:].reshape(MLA_LORA, MLA_HEADS * MLA_V).astype(BF16)
    bs = jnp.broadcast_to(b_s[:, :, None], (CMLP_GROUPS, CMLP_CHUNK, CMLP_DIM)).astype(F32)
    return dict(w_in=w_in, wqn=wqn, wqr=wqr, wkn=wkn, wv=wv, q_norm=q_norm[None, :], kv_norm=kv_norm[None, :],
                v_norm=v_norm[None, :], ws=w_s.astype(BF16), bs=bs, w_out=ab_out.astype(BF16))


def kernel(x, c, ctx, c_ctx, ada_w, ada_b, norm_mix, norm_ffn, norm_final, ffn_in, ffn_out, ab_in, ab_out,
           mla_q_norm, mla_kv_norm, mla_wq_b, mla_wkv_b, cmlp_v_norm, cmlp_ws, cmlp_bs, cd_in, cd_out,
           ret_decay_fwd, ret_decay_bwd, ret_norm, swa_sink):
    B, n, D = x.shape
    nh = ctx.shape[1]
    depth = ada_w.shape[0]
    tm_even = 512
    tm_odd = 1024
    tm_ffn = 1024
    tq_mla = 512
    tq_swa = 512

    rows = -(-(B + 1) // 8) * 8
    cond_rows = jnp.concatenate([c, c_ctx[None, :], jnp.zeros((rows - B - 1, D), F32)], axis=0)
    mods = _ada(cond_rows, ada_w, ada_b).reshape(depth, rows, 6, D)
    mod_x = mods[:, :B]
    mod_h = mods[:, B:B + 1]

    cos_x, sin_x = _rope_tables(n)
    tm_max = max(tm_even, tm_odd)
    cos_h, sin_h = jnp.ones((tm_max, LANES), F32), jnp.zeros((tm_max, LANES), F32)

    def flat(a):
        return a.reshape(1, B * nh, a.shape[-1])

    def unflat(a):
        return a.reshape(B, nh, a.shape[-1])

    pe_mask = (jnp.arange(LANES) < MLA_ROPE).astype(F32)[None, :]
    cos_xm, sin_xm = cos_x * pe_mask, sin_x * pe_mask

    h = ctx
    for layer in range(depth):
        last = layer == depth - 1
        j = layer // 2
        g_mix = norm_mix[layer][None, :]
        mx, mh = mod_x[layer], mod_h[layer]
        if layer % 2 == 0:
            w = _even_weights(ab_in[j], ab_out[j], mla_q_norm[j], mla_kv_norm[j], mla_wq_b[j], mla_wkv_b[j],
                              cmlp_v_norm[j], cmlp_ws[j], cmlp_bs[j])
            qx, kx, vx, cmx = _even_in(x, mx, g_mix, w, cos_xm, sin_xm, tm_even)
            qh, kh, vh, cmh = [unflat(a) for a in _even_in(flat(h), mh, g_mix, w, cos_h, sin_h, tm_even)]
            x = _mla_attn(qx, [(kx, vx), (kh, vh)], cmx, x, mx, w["w_out"], tq_mla)
            if not last:
                h = _mla_attn(qh, [(kh, vh)], cmh, h, mh, w["w_out"], tq_mla)
        else:
            w_in = cd_in[j].astype(BF16)
            w_out = cd_out[j].astype(BF16)
            lg = jnp.stack([jax.nn.log_sigmoid(ret_decay_fwd[j].astype(F32)),
                            jax.nn.log_sigmoid(ret_decay_bwd[j].astype(F32))])
            rn = ret_norm[j][None, :]
            rqx, rkx, rvx, rgx, sqx, skx, svx = _odd_in(x, mx, g_mix, w_in, cos_x, sin_x, tm_odd)
            rqh, rkh, rvh, rgh, sqh, skh, svh = [unflat(a) for a in
                                                 _odd_in(flat(h), mh, g_mix, w_in, cos_h, sin_h, tm_odd)]
            s0 = jnp.zeros((B, RET_HEADS, RET_QK, RET_V), F32)
            yrh, s_f, s_b = _retention(lg, rqh, rkh, rvh, rgh, s0, s0, rn)
            yrx, _, _ = _retention(lg, rqx, rkx, rvx, rgx, s_f, s_b, rn)
            sink = swa_sink[j].astype(F32)
            x = _swa(sink, sqx, (skx, svx), (skh, svh), yrx, x, mx, w_out, tq_swa)
            if not last:
                h = _swa(sink, sqh, None, (skh, svh), yrh, h, mh, w_out, tq_swa)
        g_ffn = norm_ffn[layer][None, :]
        w1 = ffn_in[layer].astype(BF16)
        w2 = ffn_out[layer].astype(BF16)
        x = _ffn(x, mx, g_ffn, w1, w2, tm_ffn, g_final=norm_final[None, :] if last else None)
        if not last:
            h = unflat(_ffn(flat(h), mh, g_ffn, w1, w2, tm_ffn))
    return x
```

```python
import functools

import numpy as np
import jax
import jax.numpy as jnp
from jax import lax
from jax.experimental import pallas as pl
from jax.experimental.pallas import tpu as pltpu

F32 = jnp.float32
BF16 = jnp.bfloat16

GRID_W = 64
ROPE_THETA = 10000.0
EPS = 1e-6
NEG_INF = -1e30
LOG2E = 1.4426950408889634

LANES = 128

MLA_HEADS = 4
MLA_LORA = 256
MLA_NOPE = 128
MLA_ROPE = 64
MLA_V = 128
MLA_QK_PAD = 256
CMLP_GROUPS = 4
CMLP_DIM = 128
CMLP_CHUNK = 128
CMLP_WIDTH = CMLP_GROUPS * CMLP_DIM
RET_HEADS = 4
RET_QK = 64
RET_V = 128
RET_CHUNK = 128
RET_QK_W = RET_HEADS * RET_QK
RET_V_W = RET_HEADS * RET_V
SWA_Q_HEADS = 8
SWA_KV_HEADS = 2
SWA_HEAD_DIM = 64
SWA_WINDOW = 128
SWA_Q_W = SWA_Q_HEADS * SWA_HEAD_DIM
SWA_KV_W = SWA_KV_HEADS * SWA_HEAD_DIM

VMEM_LIMIT_BYTES = 56 * 2**20

ODD_SUB = 256
FFN_SUB = 128
SWA_SUB = 128

_NT = (((1,), (1,)), ((), ()))
_TN = (((0,), (0,)), ((), ()))


def _dot(a, b):
    return jnp.dot(a, b, preferred_element_type=F32)


def _dot_nt(a, b):
    return lax.dot_general(a, b, _NT, preferred_element_type=F32)


def _dot_tn(a, b):
    return lax.dot_general(a, b, _TN, preferred_element_type=F32)


def _rms(x, g):
    return x * lax.rsqrt(jnp.mean(x * x, axis=-1, keepdims=True) + EPS) * g


def _rms_mod(x, g, shift, scale):
    return x * lax.rsqrt(jnp.mean(x * x, axis=-1, keepdims=True) + EPS) * (g * (1.0 + scale)) + shift


def _rope128(x, cos, sin):
    lane = lax.broadcasted_iota(jnp.int32, x.shape, 1)
    first = (lane & 32) == 0
    swapped = jnp.where(first, pltpu.roll(x, LANES - 32, 1), pltpu.roll(x, 32, 1))
    return x * cos + swapped * sin


def _rope(x, cos, sin):
    cols = [_rope128(x[:, i:i + LANES], cos, sin) for i in range(0, x.shape[1], LANES)]
    return cols[0] if len(cols) == 1 else jnp.concatenate(cols, axis=-1)


def _params(n_grid):
    return pltpu.CompilerParams(dimension_semantics=("parallel",) * n_grid,
                                vmem_limit_bytes=VMEM_LIMIT_BYTES)


def _resident(shape):
    nd = len(shape)
    return pl.BlockSpec(shape, lambda *_: (0,) * nd, pipeline_mode=pl.Buffered(1))


def _smem():
    return pl.BlockSpec(memory_space=pltpu.SMEM)


def _tok_spec(tm, width):
    return pl.BlockSpec((1, tm, width), lambda b, i: (b, i, 0))


def _seq_spec(n, width):
    return pl.BlockSpec((1, n, width), lambda b, i: (b, 0, 0))


def _mod_spec(mod):
    if mod.shape[0] == 1:
        return pl.BlockSpec((1,) + mod.shape[1:], lambda b, i: (0, 0, 0))
    return pl.BlockSpec((1,) + mod.shape[1:], lambda b, i: (b, 0, 0))


def _pos_spec(tm, table):
    nblk = table.shape[0] // tm
    return pl.BlockSpec((tm, LANES), lambda b, i: (i % nblk, 0))


def _ada_kernel(c_ref, w_ref, b_ref, o_ref):
    cond = jax.nn.silu(c_ref[...]).astype(BF16)
    o_ref[0] = _dot(cond, w_ref[0].astype(BF16)) + b_ref[0]


def _ada(cond_rows, ada_w, ada_b):
    L, D, N = ada_w.shape
    R = cond_rows.shape[0]
    tn = 1536
    return pl.pallas_call(
        _ada_kernel,
        grid=(L, N // tn),
        in_specs=[pl.BlockSpec((R, D), lambda l, j: (0, 0)),
                  pl.BlockSpec((1, D, tn), lambda l, j: (l, 0, j)),
                  pl.BlockSpec((1, 1, tn), lambda l, j: (l, 0, j))],
        out_specs=pl.BlockSpec((1, R, tn), lambda l, j: (l, 0, j)),
        out_shape=jax.ShapeDtypeStruct((L, R, N), F32),
        compiler_params=_params(2),
        name="ada_mod",
    )(cond_rows, ada_w, ada_b.reshape(L, 1, N))


def _even_in_kernel(x_ref, mod_ref, g_ref, win_ref, wqn_ref, wqr_ref, wkn_ref, wv_ref,
                    qn_ref, kvn_ref, vn_ref, ws_ref, bs_ref, cos_ref, sin_ref,
                    q_out, k_out, v_out, cm_out, *, tm, scale):
    mod = mod_ref[0]
    xn = _rms_mod(x_ref[0], g_ref[...], mod[0:1], mod[1:2]).astype(BF16)
    cos = cos_ref[...]
    sin = sin_ref[...]
    L = MLA_LORA
    kv_lat = _dot(xn, win_ref[:, 0:L])
    q_lat = _dot(xn, win_ref[:, L:2 * L])
    uv_off = 2 * L
    pe_off = uv_off + 2 * CMLP_WIDTH
    kpe = _rope128(_dot(xn, win_ref[:, pe_off:pe_off + LANES]), cos, sin).astype(BF16)
    kvn = _rms(kv_lat, kvn_ref[...]).astype(BF16)
    qn = _rms(q_lat, qn_ref[...]).astype(BF16)
    k_nope = _dot(kvn, wkn_ref[...])
    v_out[0] = _dot(kvn, wv_ref[...]).astype(BF16)
    q_nope = _dot(qn, wqn_ref[...]) * scale
    q_rope = _dot(qn, wqr_ref[...]) * scale
    for h in range(MLA_HEADS):
        src = slice(LANES * h, LANES * (h + 1))
        lo = MLA_QK_PAD * h
        q_out[0, :, lo:lo + LANES] = q_nope[:, src].astype(BF16)
        q_out[0, :, lo + LANES:lo + 2 * LANES] = _rope128(q_rope[:, src], cos, sin).astype(BF16)
        k_out[0, :, lo:lo + LANES] = k_nope[:, src].astype(BF16)
        k_out[0, :, lo + LANES:lo + 2 * LANES] = kpe
    uv = jax.nn.gelu(_dot(xn, win_ref[:, uv_off:uv_off + 2 * CMLP_WIDTH]))
    for g in range(CMLP_GROUPS):
        col = slice(CMLP_DIM * g, CMLP_DIM * (g + 1))
        vg = uv[:, CMLP_WIDTH + CMLP_DIM * g:CMLP_WIDTH + CMLP_DIM * (g + 1)]
        vg = _rms(vg, vn_ref[:, col]).astype(BF16)
        ug = uv[:, col]
        wsg = ws_ref[g]
        bsg = bs_ref[g]
        for c in range(tm // CMLP_CHUNK):
            rows = slice(CMLP_CHUNK * c, CMLP_CHUNK * (c + 1))
            mixed = _dot(wsg, vg[rows]) + bsg
            cm_out[0, rows, col] = (ug[rows] * mixed).astype(BF16)


def _even_in(xs, mod, g, w, cos, sin, tm):
    B, n, D = xs.shape
    tm = min(tm, n)
    qk_w = MLA_HEADS * MLA_QK_PAD
    v_w = MLA_HEADS * MLA_V
    scale = float((MLA_NOPE + MLA_ROPE) ** -0.5 * LOG2E)
    weights = (w["w_in"], w["wqn"], w["wqr"], w["wkn"], w["wv"], w["q_norm"], w["kv_norm"],
               w["v_norm"], w["ws"], w["bs"])
    return pl.pallas_call(
        functools.partial(_even_in_kernel, tm=tm, scale=scale),
        grid=(B, n // tm),
        in_specs=[_tok_spec(tm, D), _mod_spec(mod), _resident(g.shape)]
                 + [_resident(a.shape) for a in weights] + [_pos_spec(tm, cos), _pos_spec(tm, sin)],
        out_specs=[_tok_spec(tm, qk_w), _tok_spec(tm, qk_w), _tok_spec(tm, v_w), _tok_spec(tm, CMLP_WIDTH)],
        out_shape=[jax.ShapeDtypeStruct((B, n, qk_w), BF16), jax.ShapeDtypeStruct((B, n, qk_w), BF16),
                   jax.ShapeDtypeStruct((B, n, v_w), BF16), jax.ShapeDtypeStruct((B, n, CMLP_WIDTH), BF16)],
        compiler_params=_params(2),
        name="even_in",
    )(xs, mod, g, *weights, cos, sin)


def _mla_attn_kernel(*refs, n_pieces):
    q_ref = refs[0]
    kv_refs = refs[1:1 + 2 * n_pieces]
    cm_ref, x_ref, mod_ref, wout_ref, o_ref = refs[1 + 2 * n_pieces:]
    q = q_ref[0]
    ow = MLA_HEADS * MLA_V
    y_cm = _dot(cm_ref[0], wout_ref[ow:ow + CMLP_WIDTH, :])

    def score(h):
        qk = slice(MLA_QK_PAD * h, MLA_QK_PAD * (h + 1))
        return [_dot_nt(q[:, qk], kv_refs[2 * p][0, :, qk]) for p in range(n_pieces)]

    heads = []
    nxt = score(0)
    for h in range(MLA_HEADS):
        scores = nxt
        if h + 1 < MLA_HEADS:
            nxt = score(h + 1)
        vc = slice(MLA_V * h, MLA_V * (h + 1))
        m = functools.reduce(jnp.maximum, [jnp.max(s, axis=-1, keepdims=True) for s in scores])
        probs = [jnp.exp2(s - m) for s in scores]
        denom = sum(jnp.sum(p, axis=-1, keepdims=True) for p in probs)
        o = sum(_dot(probs[p].astype(BF16), kv_refs[2 * p + 1][0, :, vc]) for p in range(n_pieces))
        heads.append(o / denom)
    o = jnp.concatenate(heads, axis=-1).astype(BF16)
    y = _dot(o, wout_ref[0:ow, :]) + y_cm
    o_ref[0] = x_ref[0] + mod_ref[0][2:3] * y


def _mla_attn(q, pieces, cm, xs, mod, w_out, tq):
    B, n, D = xs.shape
    tq = min(tq, n)
    kv_specs, kv_args = [], []
    for k, v in pieces:
        kv_specs += [_seq_spec(k.shape[1], k.shape[2]), _seq_spec(v.shape[1], v.shape[2])]
        kv_args += [k, v]
    return pl.pallas_call(
        functools.partial(_mla_attn_kernel, n_pieces=len(pieces)),
        grid=(B, n // tq),
        in_specs=[_tok_spec(tq, q.shape[2])] + kv_specs
                 + [_tok_spec(tq, cm.shape[2]), _tok_spec(tq, D), _mod_spec(mod), _resident(w_out.shape)],
        out_specs=_tok_spec(tq, D),
        out_shape=jax.ShapeDtypeStruct((B, n, D), F32),
        compiler_params=_params(2),
        name="mla_attn",
    )(q, *kv_args, cm, xs, mod, w_out)


def _ffn_kernel(*refs, hidden, tm, sub, final):
    if final:
        x_ref, mod_ref, g_ref, win_ref, wout_ref, gf_ref, o_ref = refs
    else:
        x_ref, mod_ref, g_ref, win_ref, wout_ref, o_ref = refs
    mod = mod_ref[0]
    for r0 in range(0, tm, sub):
        rows = slice(r0, r0 + sub)
        x = x_ref[0, rows, :]
        xn = _rms_mod(x, g_ref[...], mod[3:4], mod[4:5]).astype(BF16)
        a = _dot(xn, win_ref[:, 0:hidden])
        b = _dot(xn, win_ref[:, hidden:2 * hidden])
        acc = _dot((jax.nn.silu(a) * b).astype(BF16), wout_ref[...])
        out = x + mod[5:6] * acc
        if final:
            out = _rms(out, gf_ref[...])
        o_ref[0, rows, :] = out


def _ffn(xs, mod, g, w_in, w_out, tm, g_final=None):
    B, n, D = xs.shape
    tm = min(tm, n)
    hidden = w_out.shape[0]
    final = g_final is not None
    extra = [g_final] if final else []
    return pl.pallas_call(
        functools.partial(_ffn_kernel, hidden=hidden, tm=tm, sub=min(FFN_SUB, tm), final=final),
        grid=(B, n // tm),
        in_specs=[_tok_spec(tm, D), _mod_spec(mod), _resident(g.shape), _resident(w_in.shape),
                  _resident(w_out.shape)] + [_resident(a.shape) for a in extra],
        out_specs=_tok_spec(tm, D),
        out_shape=jax.ShapeDtypeStruct((B, n, D), F32),
        compiler_params=_params(2),
        name="ffn",
    )(xs, mod, g, w_in, w_out, *extra)


def _odd_in_kernel(x_ref, mod_ref, g_ref, w_ref, cos_ref, sin_ref,
                   rq_out, rk_out, rv_out, rg_out, sq_out, sk_out, sv_out, *, tm, sub):
    mod = mod_ref[0]
    for r0 in range(0, tm, sub):
        rows = slice(r0, r0 + sub)
        xn = _rms_mod(x_ref[0, rows, :], g_ref[...], mod[0:1], mod[1:2]).astype(BF16)
        cos = cos_ref[rows, :]
        sin = sin_ref[rows, :]
        z = _dot(xn, w_ref[...])

        def proj(lo, width):
            return z[:, lo:lo + width]

        o = 0
        rk_out[0, rows, :] = (_rope(proj(o, RET_QK_W), cos, sin) * (RET_QK ** -0.5)).astype(BF16)
        o += RET_QK_W
        rv_out[0, rows, :] = proj(o, RET_V_W).astype(BF16)
        o += RET_V_W
        sk_out[0, rows, :] = _rope(proj(o, SWA_KV_W), cos, sin).astype(BF16)
        o += SWA_KV_W
        sv_out[0, rows, :] = proj(o, SWA_KV_W).astype(BF16)
        o += SWA_KV_W
        rq_out[0, rows, :] = _rope(proj(o, RET_QK_W), cos, sin).astype(BF16)
        o += RET_QK_W
        rg_out[0, rows, :] = proj(o, RET_V_W)
        o += RET_V_W
        sq_out[0, rows, :] = (_rope(proj(o, SWA_Q_W), cos, sin) * (SWA_HEAD_DIM ** -0.5 * LOG2E)).astype(BF16)


def _odd_in(xs, mod, g, w_in, cos, sin, tm):
    B, n, D = xs.shape
    tm = min(tm, n)
    widths = [(RET_QK_W, BF16), (RET_QK_W, BF16), (RET_V_W, BF16), (RET_V_W, F32),
              (SWA_Q_W, BF16), (SWA_KV_W, BF16), (SWA_KV_W, BF16)]
    return pl.pallas_call(
        functools.partial(_odd_in_kernel, tm=tm, sub=min(ODD_SUB, tm)),
        grid=(B, n // tm),
        in_specs=[_tok_spec(tm, D), _mod_spec(mod), _resident(g.shape), _resident(w_in.shape),
                  _pos_spec(tm, cos), _pos_spec(tm, sin)],
        out_specs=[_tok_spec(tm, w) for w, _ in widths],
        out_shape=[jax.ShapeDtypeStruct((B, n, w), dt) for w, dt in widths],
        compiler_params=_params(2),
        name="odd_in",
    )(xs, mod, g, w_in, cos, sin)


def _ret_kernel(lg_ref, rq_ref, rk_ref, rv_ref, rg_ref, sf0_ref, sb0_ref, rn_ref,
                yr_ref, sfo_ref, sbo_ref, mask_scr, dq_scr, dk_scr, kv_scr, s_scr, *, nc):
    C = RET_CHUNK
    HQ = RET_QK
    row = lax.broadcasted_iota(jnp.int32, (C, C), 0)
    col = lax.broadcasted_iota(jnp.int32, (C, C), 1)
    diff = (row - col).astype(F32)
    pos = row.astype(F32)
    lo = col < HQ
    c_dec = []
    for h in range(RET_HEADS):
        lf = lg_ref[0, h]
        lb = lg_ref[1, h]
        mask_scr[h] = 0.5 * (jnp.where(diff >= 0, jnp.exp(lf * jnp.maximum(diff, 0.0)), 0.0)
                             + jnp.where(diff <= 0, jnp.exp(lb * jnp.maximum(-diff, 0.0)), 0.0))
        dq_scr[h] = jnp.where(lo, jnp.exp((pos + 1.0) * lf), jnp.exp((C - pos) * lb))
        dk_scr[h] = jnp.where(lo, jnp.exp((C - 1.0 - pos) * lf), jnp.exp(pos * lb))
        c_dec.append((jnp.exp(jnp.full((1, 1), C, F32) * lf), jnp.exp(jnp.full((1, 1), C, F32) * lb)))

    def doubled(ref, rows, p):
        ab = ref[0, rows, LANES * p:LANES * (p + 1)].astype(F32)
        ba = pltpu.roll(ab, HQ, 1)
        return jnp.where(lo, ab, ba), jnp.where(lo, ba, ab)

    def chunk_kv(c, carry):
        rows = pl.ds(pl.multiple_of(c * C, C), C)
        for p in range(RET_HEADS // 2):
            for e, k2 in enumerate(doubled(rk_ref, rows, p)):
                h = 2 * p + e
                kd = (k2 * dk_scr[h]).astype(BF16)
                kv_scr[c, h] = _dot_tn(kd, rv_ref[0, rows, RET_V * h:RET_V * (h + 1)])
        return carry

    lax.fori_loop(0, nc, chunk_kv, 0, unroll=min(4, nc))

    for h in range(RET_HEADS):
        c_dec_f, c_dec_b = c_dec[h]

        def fwd(c, s, h=h, c_dec_f=c_dec_f):
            s_scr[c, h, 0:HQ, :] = s.astype(BF16)
            return s * c_dec_f + kv_scr[c, h, 0:HQ, :]

        sfo_ref[0, h] = lax.fori_loop(0, nc, fwd, sf0_ref[0, h])

        def bwd(i, s, h=h, c_dec_b=c_dec_b):
            c = nc - 1 - i
            s_scr[c, h, HQ:2 * HQ, :] = s.astype(BF16)
            return s * c_dec_b + kv_scr[c, h, HQ:2 * HQ, :]

        sbo_ref[0, h] = lax.fori_loop(0, nc, bwd, sb0_ref[0, h])

    def chunk_out(c, carry):
        rows = pl.ds(pl.multiple_of(c * C, C), C)
        for p in range(RET_HEADS // 2):
            q2s = doubled(rq_ref, rows, p)
            k2s = doubled(rk_ref, rows, p)
            for e in range(2):
                h = 2 * p + e
                vc = slice(RET_V * h, RET_V * (h + 1))
                q2 = q2s[e]
                att = (_dot_nt(q2.astype(BF16), k2s[e].astype(BF16)) * mask_scr[h]).astype(BF16)
                y = _dot(att, rv_ref[0, rows, vc]) + _dot((q2 * dq_scr[h]).astype(BF16), s_scr[c, h])
                y = _rms(y, rn_ref[:, vc]) * jax.nn.silu(rg_ref[0, rows, vc])
                yr_ref[0, rows, vc] = y.astype(BF16)
        return carry

    lax.fori_loop(0, nc, chunk_out, 0, unroll=min(4, nc))


def _retention(lg, rq, rk, rv, rg, sf0, sb0, ret_norm):
    B, n, _ = rq.shape
    nc = n // RET_CHUNK
    st_shape = (B, RET_HEADS, RET_QK, RET_V)
    st_spec = pl.BlockSpec((1, RET_HEADS, RET_QK, RET_V), lambda b: (b, 0, 0, 0))

    def seq(width):
        return pl.BlockSpec((1, n, width), lambda b: (b, 0, 0))

    return pl.pallas_call(
        functools.partial(_ret_kernel, nc=nc),
        grid=(B,),
        in_specs=[_smem(), seq(RET_QK_W), seq(RET_QK_W), seq(RET_V_W), seq(RET_V_W), st_spec, st_spec,
                  _resident(ret_norm.shape)],
        out_specs=[seq(RET_V_W), st_spec, st_spec],
        out_shape=[jax.ShapeDtypeStruct((B, n, RET_V_W), BF16), jax.ShapeDtypeStruct(st_shape, F32),
                   jax.ShapeDtypeStruct(st_shape, F32)],
        scratch_shapes=[pltpu.VMEM((RET_HEADS, RET_CHUNK, RET_CHUNK), F32),
                        pltpu.VMEM((RET_HEADS, RET_CHUNK, LANES), F32),
                        pltpu.VMEM((RET_HEADS, RET_CHUNK, LANES), F32),
                        pltpu.VMEM((nc, RET_HEADS, 2 * RET_QK, RET_V), F32),
                        pltpu.VMEM((nc, RET_HEADS, 2 * RET_QK, RET_V), BF16)],
        compiler_params=_params(1),
        name="retention",
    )(lg, rq, rk, rv, rg, sf0, sb0, ret_norm)


def _swa_kernel(*refs, tq, sub, n, local):
    if local:
        sink_ref, sq_ref, skx_ref, svx_ref, skh_ref, svh_ref, yr_ref, x_ref, mod_ref, wout_ref, o_ref = refs
    else:
        sink_ref, sq_ref, skh_ref, svh_ref, yr_ref, x_ref, mod_ref, wout_ref, o_ref = refs
    HD = SWA_HEAD_DIM
    groups = SWA_Q_HEADS // SWA_KV_HEADS
    kh = skh_ref[0]
    vh_t = svh_ref[0].astype(F32).T.astype(BF16)
    q_t = sq_ref[0].astype(F32).T.astype(BF16)
    y_ret = _dot(yr_ref[0], wout_ref[0:RET_V_W, :])
    def scores(r0):
        vx_t = None
        if local:
            base = pl.program_id(1) * tq + r0
            win = sub + 2 * SWA_WINDOW
            start = pl.multiple_of(jnp.clip(base - SWA_WINDOW, 0, n - win), SWA_WINDOW)
            kx = skx_ref[0, pl.ds(start, win), :]
            vx_t = svx_ref[0, pl.ds(start, win), :].astype(F32).T.astype(BF16)
            rel = (lax.broadcasted_iota(jnp.int32, (win, sub), 0) - lax.broadcasted_iota(jnp.int32, (win, sub), 1)
                   + (start - base))
            cap = jnp.where(jnp.abs(rel) <= SWA_WINDOW, jnp.inf, NEG_INF).astype(F32)
            cap = jnp.concatenate([cap] * groups, axis=1)
        s_hs, s_xs = [], []
        for j in range(SWA_KV_HEADS):
            kvc = slice(HD * j, HD * (j + 1))
            h0 = groups * j
            qj_t = jnp.concatenate([q_t[HD * (h0 + g):HD * (h0 + g + 1), r0:r0 + sub] for g in range(groups)], axis=1)
            s_hs.append(_dot(kh[:, kvc], qj_t))
            if local:
                s_xs.append(jnp.minimum(_dot(kx[:, kvc], qj_t), cap))
        return s_hs, s_xs, vx_t

    sinks = [jnp.concatenate([jnp.full((1, sub), sink_ref[groups * j + g] * LOG2E, F32) for g in range(groups)], axis=1)
             for j in range(SWA_KV_HEADS)]

    def finish(s_hs, s_xs, vx_t):
        ps = []
        for j in range(SWA_KV_HEADS):
            m = jnp.maximum(jnp.max(s_hs[j], axis=0, keepdims=True), sinks[j])
            if local:
                m = jnp.maximum(m, jnp.max(s_xs[j], axis=0, keepdims=True))
            p_h = jnp.exp2(s_hs[j] - m)
            denom = jnp.sum(p_h, axis=0, keepdims=True) + jnp.exp2(sinks[j] - m)
            p_x = None
            if local:
                p_x = jnp.exp2(s_xs[j] - m)
                denom = denom + jnp.sum(p_x, axis=0, keepdims=True)
                p_x = p_x.astype(BF16)
            ps.append((p_h.astype(BF16), p_x, denom))
        heads = []
        for j in range(SWA_KV_HEADS):
            kvc = slice(HD * j, HD * (j + 1))
            p_h, p_x, denom = ps[j]
            o_t = _dot(vh_t[kvc, :], p_h)
            if local:
                o_t = o_t + _dot(vx_t[kvc, :], p_x)
            o_t = o_t / denom
            heads += [o_t[:, g * sub:(g + 1) * sub] for g in range(groups)]
        return jnp.concatenate(heads, axis=0).T.astype(BF16)

    blocks = []
    starts = list(range(0, tq, sub))
    nxt = scores(starts[0])
    for idx, r0 in enumerate(starts):
        cur = nxt
        if idx + 1 < len(starts):
            nxt = scores(starts[idx + 1])
        blocks.append(finish(*cur))
    o = blocks[0] if len(blocks) == 1 else jnp.concatenate(blocks, axis=0)
    y = y_ret + _dot(o, wout_ref[RET_V_W:RET_V_W + SWA_Q_W, :])
    o_ref[0] = x_ref[0] + mod_ref[0][2:3] * y


def _swa(sink, sq, local_kv, ctx_kv, yr, xs, mod, w_out, tq):
    B, n, D = xs.shape
    tq = min(tq, n)
    local = local_kv is not None
    if local:
        assert n >= min(SWA_SUB, tq) + 2 * SWA_WINDOW
    nh = ctx_kv[0].shape[1]
    kv_specs = ([_seq_spec(n, SWA_KV_W)] * 2 if local else []) + [_seq_spec(nh, SWA_KV_W)] * 2
    kv_args = (list(local_kv) if local else []) + list(ctx_kv)
    return pl.pallas_call(
        functools.partial(_swa_kernel, tq=tq, sub=min(SWA_SUB, tq) if local else tq, n=n, local=local),
        grid=(B, n // tq),
        in_specs=[_smem(), _tok_spec(tq, SWA_Q_W)] + kv_specs
                 + [_tok_spec(tq, RET_V_W), _tok_spec(tq, D), _mod_spec(mod), _resident(w_out.shape)],
        out_specs=_tok_spec(tq, D),
        out_shape=jax.ShapeDtypeStruct((B, n, D), F32),
        compiler_params=_params(2),
        name="swa_local" if local else "swa_ctx",
    )(sink, sq, *kv_args, yr, xs, mod, w_out)


def _rope_tables(n):
    t = np.arange(n)
    row = (t // GRID_W).astype(np.float32)
    col = (t % GRID_W).astype(np.float32)
    n_freq = 16
    freqs = jnp.asarray(ROPE_THETA, F32) ** (-jnp.arange(n_freq, dtype=F32) / n_freq)
    ang = jnp.concatenate([jnp.asarray(row)[:, None] * freqs, jnp.asarray(col)[:, None] * freqs], axis=-1)
    cos, sin = jnp.cos(ang), jnp.sin(ang)
    cos2 = jnp.tile(jnp.concatenate([cos, cos], axis=-1), (1, 2))
    sin2 = jnp.tile(jnp.concatenate([-sin, sin], axis=-1), (1, 2))
    return cos2, sin2


def _even_weights(ab_in, ab_out, q_norm, kv_norm, wq_b, wkv_b, v_norm, w_s, b_s):
    kv_w = MLA_LORA + MLA_ROPE
    w_in = jnp.concatenate([ab_in[:, :MLA_LORA], ab_in[:, kv_w:kv_w + MLA_LORA], ab_in[:, kv_w + MLA_LORA:],
                            ab_in[:, MLA_LORA:kv_w], jnp.zeros((ab_in.shape[0], LANES - MLA_ROPE), ab_in.dtype)],
                           axis=1).astype(BF16)
    wq = wq_b.reshape(MLA_LORA, MLA_HEADS, MLA_NOPE + MLA_ROPE)
    wqn = wq[:, :, :MLA_NOPE].reshape(MLA_LORA, MLA_HEADS * MLA_NOPE).astype(BF16)
    wqr = jnp.pad(wq[:, :, MLA_NOPE:], ((0, 0), (0, 0), (0, LANES - MLA_ROPE)))
    wqr = wqr.reshape(MLA_LORA, MLA_HEADS * LANES).astype(BF16)
    wkv = wkv_b.reshape(MLA_LORA, MLA_HEADS, MLA_NOPE + MLA_V)
    wkn = wkv[:, :, :MLA_NOPE].reshape(MLA_LORA, MLA_HEADS * MLA_NOPE).astype(BF16)
    wv = wkv[:, :, MLA_NOPE:].reshape(MLA_LORA, MLA_HEADS * MLA_V).astype(BF16)
    bs = jnp.broadcast_to(b_s[:, :, None], (CMLP_GROUPS, CMLP_CHUNK, CMLP_DIM)).astype(F32)
    return dict(w_in=w_in, wqn=wqn, wqr=wqr, wkn=wkn, wv=wv, q_norm=q_norm[None, :], kv_norm=kv_norm[None, :],
                v_norm=v_norm[None, :], ws=w_s.astype(BF16), bs=bs, w_out=ab_out.astype(BF16))


def kernel(x, c, ctx, c_ctx, ada_w, ada_b, norm_mix, norm_ffn, norm_final, ffn_in, ffn_out, ab_in, ab_out,
           mla_q_norm, mla_kv_norm, mla_wq_b, mla_wkv_b, cmlp_v_norm, cmlp_ws, cmlp_bs, cd_in, cd_out,
           ret_decay_fwd, ret_decay_bwd, ret_norm, swa_sink):
    B, n, D = x.shape
    nh = ctx.shape[1]
    depth = ada_w.shape[0]
    tm_even = 1024
    tm_odd = 1024
    tm_ffn = 1024
    tq_mla = 512
    tq_swa = 1024

    rows = -(-(B + 1) // 8) * 8
    cond_rows = jnp.concatenate([c, c_ctx[None, :], jnp.zeros((rows - B - 1, D), F32)], axis=0)
    mods = _ada(cond_rows, ada_w, ada_b).reshape(depth, rows, 6, D)
    mod_x = mods[:, :B]
    mod_h = mods[:, B:B + 1]

    cos_x, sin_x = _rope_tables(n)
    tm_max = max(tm_even, tm_odd)
    cos_h, sin_h = jnp.ones((tm_max, LANES), F32), jnp.zeros((tm_max, LANES), F32)

    def flat(a):
        return a.reshape(1, B * nh, a.shape[-1])

    def unflat(a):
        return a.reshape(B, nh, a.shape[-1])

    pe_mask = (jnp.arange(LANES) < MLA_ROPE).astype(F32)[None, :]
    cos_xm, sin_xm = cos_x * pe_mask, sin_x * pe_mask

    h = ctx
    for layer in range(depth):
        last = layer == depth - 1
        j = layer // 2
        g_mix = norm_mix[layer][None, :]
        mx, mh = mod_x[layer], mod_h[layer]
        if layer % 2 == 0:
            w = _even_weights(ab_in[j], ab_out[j], mla_q_norm[j], mla_kv_norm[j], mla_wq_b[j], mla_wkv_b[j],
                              cmlp_v_norm[j], cmlp_ws[j], cmlp_bs[j])
            qx, kx, vx, cmx = _even_in(x, mx, g_mix, w, cos_xm, sin_xm, tm_even)
            qh, kh, vh, cmh = [unflat(a) for a in _even_in(flat(h), mh, g_mix, w, cos_h, sin_h, tm_even)]
            x = _mla_attn(qx, [(kx, vx), (kh, vh)], cmx, x, mx, w["w_out"], tq_mla)
            if not last:
                h = _mla_attn(qh, [(kh, vh)], cmh, h, mh, w["w_out"], tq_mla)
        else:
            w_in = cd_in[j].astype(BF16)
            w_out = cd_out[j].astype(BF16)
            lg = jnp.stack([jax.nn.log_sigmoid(ret_decay_fwd[j].astype(F32)),
                            jax.nn.log_sigmoid(ret_decay_bwd[j].astype(F32))])
            rn = ret_norm[j][None, :]
            rqx, rkx, rvx, rgx, sqx, skx, svx = _odd_in(x, mx, g_mix, w_in, cos_x, sin_x, tm_odd)
            rqh, rkh, rvh, rgh, sqh, skh, svh = [unflat(a) for a in
                                                 _odd_in(flat(h), mh, g_mix, w_in, cos_h, sin_h, tm_odd)]
            s0 = jnp.zeros((B, RET_HEADS, RET_QK, RET_V), F32)
            yrh, s_f, s_b = _retention(lg, rqh, rkh, rvh, rgh, s0, s0, rn)
            yrx, _, _ = _retention(lg, rqx, rkx, rvx, rgx, s_f, s_b, rn)
            sink = swa_sink[j].astype(F32)
            x = _swa(sink, sqx, (skx, svx), (skh, svh), yrx, x, mx, w_out, tq_swa)
            if not last:
                h = _swa(sink, sqh, None, (skh, svh), yrh, h, mh, w_out, tq_swa)
        g_ffn = norm_ffn[layer][None, :]
        w1 = ffn_in[layer].astype(BF16)
        w2 = ffn_out[layer].astype(BF16)
        x = _ffn(x, mx, g_ffn, w1, w2, tm_ffn, g_final=norm_final[None, :] if last else None)
        if not last:
            h = unflat(_ffn(flat(h), mh, g_ffn, w1, w2, tm_ffn))
    return x
```
